```python
import math
import jax
import jax.numpy as jnp
from jax import lax
import numpy as np

D_MODEL = 1024
BATCH = 4
SEQ = 4096
DEPTH = 4
DEC_BATCH = 128
DEC_SEQ = 4
PAST_LEN = 8192
PAGE_SIZE = 128

N_BRANCH = 4
BRANCH_W = D_MODEL // 4
EPS = 1e-6
RW_HEADS = 4
RW_HD = 64
RW_W = RW_HEADS * RW_HD
RW_W_LORA = 32
RW_A_LORA = 32
RW_V_LORA = 16
RW_G_LORA = 64
RW_GN_EPS = 64e-5
RW_COLS = 3 * RW_W + RW_W_LORA + RW_A_LORA + RW_G_LORA
GLA_HEADS = 4
GLA_DK = 32
GLA_DV = 64
GLA_W = GLA_HEADS * GLA_DV
GLA_A_LORA = 16
GLA_TAU = 16.0
GLA_CHUNK = 16
GLA_COLS = 2 * GLA_HEADS * GLA_DK + GLA_W + GLA_A_LORA + GLA_W
S5_GROUP = 16
S5_GROUPS = BRANCH_W // S5_GROUP
S5_P = 64
S5_W = S5_GROUPS * S5_GROUP
S5_COLS = S5_W
MLA_HEADS = 4
MLA_NOPE = 64
MLA_ROPE = 32
MLA_QK = MLA_NOPE + MLA_ROPE
MLA_V = 64
MLA_W = MLA_HEADS * MLA_V
MLA_Q_LORA = 256
MLA_KV_LORA = 256
MLA_COLS = MLA_Q_LORA + MLA_KV_LORA + MLA_ROPE
MLA_SCALE = MLA_QK ** -0.5
ROPE_THETA = 10000.0
Q_BLOCK = 128
NEG_INF = -1e30
IN_COLS = RW_COLS + GLA_COLS + S5_COLS + MLA_COLS
D_FF = 4 * D_MODEL

kernel_name = 'hybrid_rwkv7_gla_s5_mla_decode_step'


def rmsnorm(x, g):
    xf = x.astype(jnp.float32)
    return xf * lax.rsqrt(jnp.mean(xf * xf, axis=-1, keepdims=True) + EPS) * g


def split_cols(x, sizes):
    return jnp.split(x, [int(o) for o in np.cumsum(sizes)[:-1]], axis=-1)


def rope(x, pos):
    half = MLA_ROPE // 2
    inv = ROPE_THETA ** (-jnp.arange(half, dtype=jnp.float32) / half)
    ang = pos.astype(jnp.float32)[:, None] * inv
    shp = (ang.shape[0],) + (1,) * (x.ndim - 3) + (half,)
    cos, sin = jnp.cos(ang).reshape(shp), jnp.sin(ang).reshape(shp)
    x1, x2 = x[..., :half], x[..., half:]
    return jnp.concatenate([x1 * cos - x2 * sin, x2 * cos + x1 * sin], axis=-1)


def rwkv7_scan(r, decay, k, v, kk, a, S0):
    f32 = jnp.float32
    seq = tuple(jnp.moveaxis(t.astype(f32), 1, 0) for t in (r, decay, k, v, kk, a))

    def step(S, inp):
        r_t, d_t, k_t, v_t, kk_t, a_t = inp
        sa = jnp.einsum('bhij,bhj->bhi', S, -kk_t)
        S = (S * d_t[:, :, None, :] + sa[..., None] * (kk_t * a_t)[:, :, None, :]
             + v_t[..., None] * k_t[:, :, None, :])
        return S, jnp.einsum('bhij,bhj->bhi', S, r_t)

    S_fin, ys = lax.scan(step, S0.astype(f32), seq)
    return jnp.moveaxis(ys, 0, 1), S_fin


def rwkv7_branch(rw, layer, v_first, S0, p):
    B, T, _ = rw.shape
    r, k, v, wd, ad, gd = split_cols(rw, [RW_W, RW_W, RW_W, RW_W_LORA, RW_A_LORA, RW_G_LORA])
    w_log = -jax.nn.softplus(-(p['rw_w0'] + jnp.tanh(wd) @ p['rw_w2'])) - 0.5
    decay = jnp.exp(-jnp.exp(w_log.astype(jnp.float32)))
    a = jax.nn.sigmoid(p['rw_a0'] + ad @ p['rw_a2'])
    g = jax.nn.sigmoid(gd) @ p['rw_g2']
    if layer == 0:
        v_first = v
    else:
        v = v + (v_first - v) * jax.nn.sigmoid(p['rw_v0'] + (v @ p['rw_v1']) @ p['rw_v2'])

    def hd(t):
        return t.reshape(B, T, RW_HEADS, RW_HD)

    kk = hd(k * p['rw_k_k']).astype(jnp.float32)
    kk = kk * lax.rsqrt(jnp.sum(kk * kk, axis=-1, keepdims=True) + 1e-12)
    k = k * (1.0 + (a - 1.0) * p['rw_k_a'])
    y, S = rwkv7_scan(hd(r), hd(decay), hd(k), hd(v), kk, hd(a), S0)
    mu = jnp.mean(y, axis=-1, keepdims=True)
    var = jnp.mean(jnp.square(y - mu), axis=-1, keepdims=True)
    yn = ((y - mu) * lax.rsqrt(var + RW_GN_EPS)).reshape(B, T, RW_W) * p['rw_ln_g'] + p['rw_ln_b']
    bonus = jnp.sum(hd(r) * hd(k) * p['rw_r_k'], axis=-1, keepdims=True) * hd(v)
    return (yn + bonus.reshape(B, T, RW_W)) * g, S, v_first


def gla_chunked(q, k, v, log_a, S0):
    f32 = jnp.float32
    B, T, H, _ = q.shape
    DV = v.shape[-1]
    C = min(GLA_CHUNK, T)
    n = -(-T // C)
    pad = n * C - T

    def prep(t):
        t = jnp.pad(t.astype(f32), ((0, 0), (0, pad), (0, 0), (0, 0)))
        return t.reshape(B, n, C, H, t.shape[-1]).transpose(0, 1, 3, 2, 4)

    q, k, v, log_a = prep(q), prep(k), prep(v), prep(log_a)
    b = jnp.cumsum(log_a, axis=3)
    causal = jnp.tril(jnp.ones((C, C), dtype=bool))
    diff = b[..., :, None, :] - b[..., None, :, :]
    dec = jnp.exp(jnp.where(causal[:, :, None], diff, -jnp.inf))
    A = jnp.einsum('bnhtd,bnhsd,bnhtsd->bnhts', q, k, dec)
    o = jnp.einsum('bnhts,bnhsv->bnhtv', A, v)
    b_last = b[..., -1:, :]
    U = jnp.einsum('bnhsd,bnhsv->bnhdv', k * jnp.exp(b_last - b), v)
    Dc = jnp.exp(b_last[..., 0, :])

    def step(S, inp):
        d, u = inp
        return d[..., None] * S + u, S

    S_fin, S_prev = lax.scan(step, S0.astype(f32), (jnp.moveaxis(Dc, 1, 0), jnp.moveaxis(U, 1, 0)))
    S_prev = jnp.moveaxis(S_prev, 0, 1)
    o = o + jnp.einsum('bnhtd,bnhdv->bnhtv', q * jnp.exp(b), S_prev)
    o = o.transpose(0, 1, 3, 2, 4).reshape(B, n * C, H, DV)[:, :T]
    return o, S_fin


def gla_branch(cols, S0, p):
    B, T, _ = cols.shape
    q, k, v, ad, rg = split_cols(cols, [GLA_HEADS * GLA_DK, GLA_HEADS * GLA_DK, GLA_W, GLA_A_LORA, GLA_W])
    log_a = jax.nn.log_sigmoid((ad @ p['gla_a2'] + p['gla_ab']).astype(jnp.float32)) / GLA_TAU
    o, S = gla_chunked(q.reshape(B, T, GLA_HEADS, GLA_DK) * GLA_DK ** -0.5,
                       k.reshape(B, T, GLA_HEADS, GLA_DK),
                       v.reshape(B, T, GLA_HEADS, GLA_DV),
                       log_a.reshape(B, T, GLA_HEADS, GLA_DK), S0)
    o = rmsnorm(o, p['gla_norm_g']).reshape(B, T, GLA_W)
    return o * jax.nn.silu(rg), S


def complex_affine_combine(e1, e2):
    a1r, a1i, b1r, b1i = e1
    a2r, a2i, b2r, b2i = e2
    return (a1r * a2r - a1i * a2i, a1r * a2i + a1i * a2r,
            a2r * b1r - a2i * b1i + b2r, a2r * b1i + a2i * b1r + b2i)


def s5_branch(u, h0_re, h0_im, p):
    f32 = jnp.float32
    B, T, _ = u.shape
    u = u.astype(f32).reshape(B, T, S5_GROUPS, S5_GROUP)
    a_re, a_im = p['s5_a_re'].astype(f32), p['s5_a_im'].astype(f32)
    dt = jnp.exp(p['s5_log_dt'].astype(f32))[:, None]
    mag = jnp.exp(a_re * dt)
    ab_re, ab_im = mag * jnp.cos(a_im * dt), mag * jnp.sin(a_im * dt)
    den = a_re * a_re + a_im * a_im
    cr = ((ab_re - 1.0) * a_re + ab_im * a_im) / den
    ci = (ab_im * a_re - (ab_re - 1.0) * a_im) / den
    b_re, b_im = p['s5_b_re'], p['s5_b_im']
    bb_re = cr[..., None] * b_re - ci[..., None] * b_im
    bb_im = cr[..., None] * b_im + ci[..., None] * b_re
    bu_re = jnp.einsum('btgi,gpi->btgp', u, bb_re)
    bu_im = jnp.einsum('btgi,gpi->btgp', u, bb_im)
    h0_re, h0_im = h0_re.astype(f32), h0_im.astype(f32)
    bu_re = bu_re.at[:, 0].add(ab_re * h0_re - ab_im * h0_im)
    bu_im = bu_im.at[:, 0].add(ab_re * h0_im + ab_im * h0_re)
    el = (jnp.broadcast_to(ab_re, bu_re.shape), jnp.broadcast_to(ab_im, bu_re.shape), bu_re, bu_im)
    _, _, h_re, h_im = lax.associative_scan(complex_affine_combine, el, axis=1)
    y = (jnp.einsum('btgp,gip->btgi', h_re, p['s5_c_re'])
         - jnp.einsum('btgp,gip->btgi', h_im, p['s5_c_im']) + p['s5_d'] * u)
    z = jax.nn.gelu(y.reshape(B, T, S5_W))
    return (z @ p['s5_glu_a']) * jax.nn.sigmoid(z @ p['s5_glu_b']), h_re[:, -1], h_im[:, -1]


def mla_attend(q_nope, q_rope, q_pos, k_nope, k_rope, k_inv, k_pos, c_all):
    s = (jnp.einsum('bqhd,bkhd->bhqk', q_nope, k_nope)
         + jnp.einsum('bqhd,bkd->bhqk', q_rope, k_rope))
    s = s.astype(jnp.float32) * (jnp.swapaxes(k_inv, 1, 2)[:, :, None, :] * MLA_SCALE)
    s = jnp.where(k_pos[None, :] <= q_pos[:, None], s, NEG_INF)
    prob = jax.nn.softmax(s, axis=-1)
    return jnp.einsum('bhqk,bkr->bqhr', prob, c_all)


def mla_branch(cols, pos, past_c, past_kr, p):
    f32 = jnp.float32
    B, T, _ = cols.shape
    qd, kvd, kr = split_cols(cols, [MLA_Q_LORA, MLA_KV_LORA, MLA_ROPE])
    q = (rmsnorm(qd, p['mla_qa_g']) @ p['mla_w_uq']).reshape(B, T, MLA_HEADS, MLA_QK)
    q = rmsnorm(q, p['mla_q_g'])
    q_nope = q[..., :MLA_NOPE] * p['mla_k_g'][:MLA_NOPE]
    q_rope = rope(q[..., MLA_NOPE:], pos)
    c = rmsnorm(kvd, p['mla_kva_g'])
    kr = kr.astype(f32)
    if past_c is None:
        c_all, kr_all = c, kr
    else:
        c_all = jnp.concatenate([past_c.astype(f32), c], axis=1)
        kr_all = jnp.concatenate([past_kr.astype(f32), kr], axis=1)
    k_pos = jnp.arange(c_all.shape[1], dtype=jnp.int32)
    k_nope = jnp.einsum('bkr,rhd->bkhd', c_all, p['mla_w_uk'])
    ssq = jnp.sum(k_nope * k_nope, axis=-1) + jnp.sum(kr_all * kr_all, axis=-1)[:, :, None]
    k_inv = lax.rsqrt(ssq / MLA_QK + EPS)
    k_rope = rope(kr_all * p['mla_k_g'][MLA_NOPE:], k_pos)
    if T > Q_BLOCK and T % Q_BLOCK == 0:
        nb = T // Q_BLOCK
        qnb = q_nope.reshape(B, nb, Q_BLOCK, MLA_HEADS, MLA_NOPE).transpose(1, 0, 2, 3, 4)
        qrb = q_rope.reshape(B, nb, Q_BLOCK, MLA_HEADS, MLA_ROPE).transpose(1, 0, 2, 3, 4)
        posb = pos.reshape(nb, Q_BLOCK)

        def blk(args):
            qn_b, qr_b, pos_b = args
            return mla_attend(qn_b, qr_b, pos_b, k_nope, k_rope, k_inv, k_pos, c_all)

        o_lat = lax.map(blk, (qnb, qrb, posb))
        o_lat = o_lat.transpose(1, 0, 2, 3, 4).reshape(B, T, MLA_HEADS, MLA_KV_LORA)
    else:
        o_lat = mla_attend(q_nope, q_rope, pos, k_nope, k_rope, k_inv, k_pos, c_all)
    o = jnp.einsum('bqhr,rhd->bqhd', o_lat, p['mla_w_uv']).reshape(B, T, MLA_W)
    return o, c, kr


def mixer_sublayer(x, pos, layer, v_first, p, st):
    S_rw0, shift0, S_gla0, h0_re, h0_im, past_c, past_kr = st
    B, T, _ = x.shape
    xn = rmsnorm(x, p['norm1_g'])
    proj = xn @ p['w_in']
    rw_p, gla_p, s5_u, mla_p = split_cols(proj, [RW_COLS, GLA_COLS, S5_COLS, MLA_COLS])
    xs = jnp.concatenate([shift0[:, None, :].astype(xn.dtype), xn[:, :-1]], axis=1)
    rw_s = xs @ p['w_in'][:, :RW_COLS]
    rw = rw_p + (rw_s - rw_p) * p['rw_mu']
    y_rw, S_rw, v_first = rwkv7_branch(rw, layer, v_first, S_rw0, p)
    y_gla, S_gla = gla_branch(gla_p, S_gla0, p)
    y_s5, h_re, h_im = s5_branch(s5_u, h0_re, h0_im, p)
    y_mla, c_new, kr_new = mla_branch(mla_p, pos, past_c, past_kr, p)
    branches = jnp.stack([y_rw, y_gla, y_s5, y_mla], axis=2)
    gates = jax.nn.sigmoid((xn @ p['w_gate']).astype(jnp.float32)).reshape(B, T, N_BRANCH, D_MODEL)
    up = jnp.einsum('btnc,ncd->btnd', branches, p['w_branch'])
    merged = jnp.einsum('btnd,btnd->btd', gates, up)
    new_state = (S_rw, xn[:, -1], S_gla, h_re, h_im, c_new, kr_new)
    return x + merged @ p['w_out'], v_first, new_state


def mlp_sublayer(x, p):
    h = rmsnorm(x, p['norm2_g']) @ p['w_ff1']
    return x + jnp.square(jax.nn.relu(h)) @ p['w_ff2']


def stack_field(outs, i):
    return jnp.stack([o[i] for o in outs], axis=0)


def setup_inputs(seed: int = 0) -> dict:
    key = jax.random.key(seed)
    ks = iter(jax.random.split(key, 64))
    f32 = jnp.float32

    def nrm(shape, scale=1.0):
        return jax.random.normal(next(ks), shape, f32) * scale

    def gain(shape):
        return 1.0 + nrm(shape, 0.02)

    L = DEPTH
    n_pages = PAST_LEN // PAGE_SIZE
    n_used = DEC_BATCH * n_pages
    n_pool = n_used + max(1, n_used // 4)
    page_table = jax.random.permutation(next(ks), n_pool)[:n_used].reshape(DEC_BATCH, n_pages).astype(jnp.int32)
    n_idx = jnp.arange(S5_P, dtype=f32)
    log_dt_min, log_dt_max = math.log(1e-3), math.log(1e-1)
    return {
        'x_prompt': nrm((BATCH, SEQ, D_MODEL)),
        'x_sample': nrm((DEC_BATCH, DEC_SEQ, D_MODEL)),
        'cache_mla_latent': nrm((L, n_pool, PAGE_SIZE, MLA_KV_LORA)),
        'cache_mla_krope': nrm((L, n_pool, PAGE_SIZE, MLA_ROPE)),
        'page_table': page_table,
        'state_rwkv': nrm((L, DEC_BATCH, RW_HEADS, RW_HD, RW_HD), 0.5),
        'state_shift': nrm((L, DEC_BATCH, D_MODEL)),
        'state_gla': nrm((L, DEC_BATCH, GLA_HEADS, GLA_DK, GLA_DV), 0.5),
        'state_s5_re': nrm((L, DEC_BATCH, S5_GROUPS, S5_P), 0.3),
        'state_s5_im': nrm((L, DEC_BATCH, S5_GROUPS, S5_P), 0.3),
        'norm1_g': gain((L, D_MODEL)),
        'w_in': nrm((L, D_MODEL, IN_COLS), D_MODEL ** -0.5),
        'rw_mu': jax.random.uniform(next(ks), (L, RW_COLS), f32),
        'rw_w0': -1.5 + nrm((L, RW_W), 0.5),
        'rw_w2': nrm((L, RW_W_LORA, RW_W), 0.1 * RW_W_LORA ** -0.5),
        'rw_a0': nrm((L, RW_W), 0.1),
        'rw_a2': nrm((L, RW_A_LORA, RW_W), RW_A_LORA ** -0.5),
        'rw_g2': nrm((L, RW_G_LORA, RW_W), RW_G_LORA ** -0.5),
        'rw_v0': nrm((L - 1, RW_W), 0.1),
        'rw_v1': nrm((L - 1, RW_W, RW_V_LORA), RW_W ** -0.5),
        'rw_v2': nrm((L - 1, RW_V_LORA, RW_W), RW_V_LORA ** -0.5),
        'rw_k_k': 0.85 + nrm((L, RW_W), 0.02),
        'rw_k_a': gain((L, RW_W)),
        'rw_r_k': nrm((L, RW_HEADS, RW_HD), 0.1),
        'rw_ln_g': gain((L, RW_W)),
        'rw_ln_b': nrm((L, RW_W), 0.02),
        'gla_a2': nrm((L, GLA_A_LORA, GLA_HEADS * GLA_DK), GLA_A_LORA ** -0.5),
        'gla_ab': nrm((L, GLA_HEADS * GLA_DK), 0.1),
        'gla_norm_g': gain((L, GLA_DV)),
        's5_a_re': -0.5 + nrm((L, S5_GROUPS, S5_P), 0.01),
        's5_a_im': math.pi * n_idx + nrm((L, S5_GROUPS, S5_P), 0.01),
        's5_log_dt': log_dt_min + jax.random.uniform(next(ks), (L, S5_GROUPS), f32) * (log_dt_max - log_dt_min),
        's5_b_re': nrm((L, S5_GROUPS, S5_P, S5_GROUP), (2.0 * S5_GROUP) ** -0.5),
        's5_b_im': nrm((L, S5_GROUPS, S5_P, S5_GROUP), (2.0 * S5_GROUP) ** -0.5),
        's5_c_re': nrm((L, S5_GROUPS, S5_GROUP, S5_P), (2.0 * S5_P) ** -0.5),
        's5_c_im': nrm((L, S5_GROUPS, S5_GROUP, S5_P), (2.0 * S5_P) ** -0.5),
        's5_d': nrm((L, S5_GROUPS, S5_GROUP)),
        's5_glu_a': nrm((L, S5_W, S5_W), S5_W ** -0.5),
        's5_glu_b': nrm((L, S5_W, S5_W), S5_W ** -0.5),
        'mla_qa_g': gain((L, MLA_Q_LORA)),
        'mla_w_uq': nrm((L, MLA_Q_LORA, MLA_HEADS * MLA_QK), MLA_Q_LORA ** -0.5),
        'mla_kva_g': gain((L, MLA_KV_LORA)),
        'mla_w_uk': nrm((L, MLA_KV_LORA, MLA_HEADS, MLA_NOPE), MLA_KV_LORA ** -0.5),
        'mla_w_uv': nrm((L, MLA_KV_LORA, MLA_HEADS, MLA_V), MLA_KV_LORA ** -0.5),
        'mla_q_g': gain((L, MLA_QK)),
        'mla_k_g': gain((L, MLA_QK)),
        'w_gate': nrm((L, D_MODEL, N_BRANCH * D_MODEL), D_MODEL ** -0.5),
        'w_branch': nrm((L, N_BRANCH, BRANCH_W, D_MODEL), BRANCH_W ** -0.5),
        'w_out': nrm((L, D_MODEL, D_MODEL), D_MODEL ** -0.5),
        'norm2_g': gain((L, D_MODEL)),
        'w_ff1': nrm((L, D_MODEL, D_FF), D_MODEL ** -0.5),
        'w_ff2': nrm((L, D_FF, D_MODEL), D_FF ** -0.5),
    }


def reference(x_prompt, x_sample, cache_mla_latent, cache_mla_krope, page_table,
              state_rwkv, state_shift, state_gla, state_s5_re, state_s5_im,
              norm1_g, w_in, rw_mu, rw_w0, rw_w2, rw_a0, rw_a2, rw_g2, rw_v0, rw_v1, rw_v2,
              rw_k_k, rw_k_a, rw_r_k, rw_ln_g, rw_ln_b, gla_a2, gla_ab, gla_norm_g,
              s5_a_re, s5_a_im, s5_log_dt, s5_b_re, s5_b_im, s5_c_re, s5_c_im, s5_d, s5_glu_a, s5_glu_b,
              mla_qa_g, mla_w_uq, mla_kva_g, mla_w_uk, mla_w_uv, mla_q_g, mla_k_g,
              w_gate, w_branch, w_out, norm2_g, w_ff1, w_ff2):
    f32 = jnp.float32
    Bp, Tp, _ = x_prompt.shape
    Bs, Ts, _ = x_sample.shape
    past_len = page_table.shape[1] * PAGE_SIZE
    pos_p = jnp.arange(Tp, dtype=jnp.int32)
    pos_s = past_len + jnp.arange(Ts, dtype=jnp.int32)
    zero_state_p = (jnp.zeros((Bp, RW_HEADS, RW_HD, RW_HD), f32), jnp.zeros((Bp, D_MODEL), f32),
                    jnp.zeros((Bp, GLA_HEADS, GLA_DK, GLA_DV), f32),
                    jnp.zeros((Bp, S5_GROUPS, S5_P), f32), jnp.zeros((Bp, S5_GROUPS, S5_P), f32),
                    None, None)
    xp, xs = x_prompt, x_sample
    vf_p, vf_s = None, None
    outs_p, outs_s = [], []
    for l in range(DEPTH):
        p = {'norm1_g': norm1_g[l], 'w_in': w_in[l], 'rw_mu': rw_mu[l], 'rw_w0': rw_w0[l],
             'rw_w2': rw_w2[l], 'rw_a0': rw_a0[l], 'rw_a2': rw_a2[l], 'rw_g2': rw_g2[l],
             'rw_k_k': rw_k_k[l], 'rw_k_a': rw_k_a[l], 'rw_r_k': rw_r_k[l],
             'rw_ln_g': rw_ln_g[l], 'rw_ln_b': rw_ln_b[l],
             'gla_a2': gla_a2[l], 'gla_ab': gla_ab[l], 'gla_norm_g': gla_norm_g[l],
             's5_a_re': s5_a_re[l], 's5_a_im': s5_a_im[l], 's5_log_dt': s5_log_dt[l],
             's5_b_re': s5_b_re[l], 's5_b_im': s5_b_im[l], 's5_c_re': s5_c_re[l], 's5_c_im': s5_c_im[l],
             's5_d': s5_d[l], 's5_glu_a': s5_glu_a[l], 's5_glu_b': s5_glu_b[l],
             'mla_qa_g': mla_qa_g[l], 'mla_w_uq': mla_w_uq[l], 'mla_kva_g': mla_kva_g[l],
             'mla_w_uk': mla_w_uk[l], 'mla_w_uv': mla_w_uv[l], 'mla_q_g': mla_q_g[l], 'mla_k_g': mla_k_g[l],
             'w_gate': w_gate[l], 'w_branch': w_branch[l], 'w_out': w_out[l],
             'norm2_g': norm2_g[l], 'w_ff1': w_ff1[l], 'w_ff2': w_ff2[l]}
        if l > 0:
            p['rw_v0'] = rw_v0[l - 1]
            p['rw_v1'] = rw_v1[l - 1]
            p['rw_v2'] = rw_v2[l - 1]
        past_c = cache_mla_latent[l][page_table].reshape(Bs, past_len, MLA_KV_LORA)
        past_kr = cache_mla_krope[l][page_table].reshape(Bs, past_len, MLA_ROPE)
        st_s = (state_rwkv[l], state_shift[l], state_gla[l], state_s5_re[l], state_s5_im[l], past_c, past_kr)
        xp, vf_p, sp = mixer_sublayer(xp, pos_p, l, vf_p, p, zero_state_p)
        xs, vf_s, ss = mixer_sublayer(xs, pos_s, l, vf_s, p, st_s)
        xp = mlp_sublayer(xp, p)
        xs = mlp_sublayer(xs, p)
        outs_p.append(sp)
        outs_s.append(ss)
    return (xp, xs,
            stack_field(outs_p, 5), stack_field(outs_p, 6),
            stack_field(outs_s, 5), stack_field(outs_s, 6),
            stack_field(outs_p, 0), stack_field(outs_s, 0),
            stack_field(outs_p, 1), stack_field(outs_s, 1),
            stack_field(outs_p, 2), stack_field(outs_s, 2),
            stack_field(outs_p, 3), stack_field(outs_p, 4),
            stack_field(outs_s, 3), stack_field(outs_s, 4))
```

```python
import functools
import math

import jax
import jax.numpy as jnp
import numpy as np
from jax import lax
from jax.experimental import pallas as pl
from jax.experimental.pallas import tpu as pltpu

F32 = jnp.float32
BF16 = jnp.bfloat16

D_MODEL = 1024
N_BRANCH = 4
BRANCH_W = 256
EPS = 1e-6
PAGE_SIZE = 128
RW_HEADS, RW_HD, RW_W = 4, 64, 256
RW_COLS = 896
RW_GN_EPS = 64e-5
GLA_HEADS, GLA_DK, GLA_DV, GLA_W = 4, 32, 64, 256
GLA_KW = GLA_HEADS * GLA_DK
GLA_A_LORA = 16
GLA_TAU = 16.0
GLA_COLS = 784
GLA_PAD = 896
S5_GROUP, S5_GROUPS, S5_P, S5_W = 16, 16, 64, 256
S5_STATE = S5_GROUPS * S5_P
MLA_HEADS, MLA_NOPE, MLA_ROPE, MLA_QK, MLA_V = 4, 64, 32, 96, 64
MLA_LORA = 256
MLA_COLS = 544
MLA_PAD = 640
MLA_HP = 128
MLA_HW = MLA_HEADS * MLA_HP
MLA_SCALE = MLA_QK ** -0.5
ROPE_THETA = 10000.0
NEG_INF = -1e30
D_FF = 4096
IN_PAD = RW_COLS + GLA_PAD + S5_W + MLA_PAD

V7X_VMEM_BYTES = 64 * 1024 * 1024
VMEM_LIMIT = V7X_VMEM_BYTES - 8 * 1024 * 1024
LANE = 128
SUBLANE = 8

ROW_TILE = 512
RW_CHUNK = 64
GLA_SUB = 16
S5_TILE = 128
ATT_TQ = 512
ATT_TK = 512
DEC_PAGES = 8


def _cparams(sem):
    return pltpu.CompilerParams(dimension_semantics=sem, vmem_limit_bytes=VMEM_LIMIT)


def _bf(x):
    return x.astype(BF16)


def _dg(a, b, ca, cb):
    return lax.dot_general(a, b, (((ca,), (cb,)), ((), ())), preferred_element_type=F32)


def _mm(a, b, ca=1, cb=0):
    return _dg(_bf(a), _bf(b), ca, cb)


def _split(x):
    hi = _bf(x)
    return hi, _bf(x - hi.astype(F32))


def _mm3(a, b, ca=1, cb=0):
    ah, al = _split(a)
    bh, bl = _split(b)
    return _dg(ah, bh, ca, cb) + _dg(al, bh, ca, cb) + _dg(ah, bl, ca, cb)


def _mm2(a, b01, ca=1, cb=0):
    ah, al = _split(a)
    return _dg(ah, b01, ca, cb) + _dg(al, b01, ca, cb)


def _mm2l(a01, b, ca=1, cb=0):
    bh, bl = _split(b)
    return _dg(a01, bh, ca, cb) + _dg(a01, bl, ca, cb)


def _sigmoid(x):
    return 1.0 / (1.0 + jnp.exp(-x))


def _softplus(x):
    return jnp.maximum(x, 0.0) + jnp.log(1.0 + jnp.exp(-jnp.abs(x)))


def _iota(shape, dim):
    return lax.broadcasted_iota(jnp.int32, shape, dim)


def _block_ones(n, blk, dtype=BF16):
    return (_iota((n, n), 0) // blk == _iota((n, n), 1) // blk).astype(dtype)


def _full(shape):
    nd = len(shape)
    return pl.BlockSpec(shape, lambda *_: (0,) * nd)


def _norm_proj_kernel(x_ref, g_ref, w_ref, xn_ref, rw_ref, gla_ref, s5_ref, mla_ref):
    x = x_ref[...]
    xn = x * lax.rsqrt(jnp.mean(x * x, axis=-1, keepdims=True) + EPS) * g_ref[...]
    xn_ref[...] = xn
    xb = _bf(xn)
    o = 0
    for ref in (rw_ref, gla_ref, s5_ref, mla_ref):
        w = ref.shape[1]
        ref[...] = jnp.dot(xb, w_ref[:, o:o + w], preferred_element_type=F32)
        o += w


def norm_proj(x, g, w_all):
    n = x.shape[0]
    widths = (RW_COLS, GLA_PAD, S5_W, MLA_PAD)
    row = lambda w: pl.BlockSpec((ROW_TILE, w), lambda i: (i, 0))
    return pl.pallas_call(
        _norm_proj_kernel,
        grid=(n // ROW_TILE,),
        in_specs=[row(D_MODEL), _full((1, D_MODEL)), _full((D_MODEL, IN_PAD))],
        out_specs=[row(D_MODEL)] + [row(w) for w in widths],
        out_shape=[jax.ShapeDtypeStruct((n, D_MODEL), F32)]
        + [jax.ShapeDtypeStruct((n, w), F32) for w in widths],
        compiler_params=_cparams(("parallel",)),
    )(x, g.reshape(1, D_MODEL), w_all)


def _matmul_kernel(a_ref, b_ref, o_ref):
    o_ref[...] = _mm(a_ref[...], b_ref[...])


def small_matmul(a, b):
    m, n = a.shape[0], b.shape[1]
    return pl.pallas_call(
        _matmul_kernel,
        out_shape=jax.ShapeDtypeStruct((m, n), F32),
        compiler_params=_cparams(None),
    )(a, b)


def _tri_inverse(a, c):
    eye = (_iota((c, c), 0) == _iota((c, c), 1)).astype(F32)
    x = eye - a
    p = a
    n = 2
    while n < c:
        p = _mm3(p, p)
        x = x + _mm3(x, p)
        n *= 2
    return x


def _rwkv_kernel(has_vres, c, t_valid, *refs):
    if has_vres:
        (prw_ref, sp_ref, s0_ref, vf_ref, mu_ref, w0_ref, w2_ref, a0_ref, a2_ref, g2_ref,
         v0_ref, v1_ref, v2_ref, kk_ref, ka_ref, rk_ref, lng_ref, lnb_ref,
         y_ref, s_ref, carry_ref) = refs
    else:
        (prw_ref, sp_ref, s0_ref, mu_ref, w0_ref, w2_ref, a0_ref, a2_ref, g2_ref,
         kk_ref, ka_ref, rk_ref, lng_ref, lnb_ref,
         y_ref, vout_ref, s_ref, carry_ref) = refs

    @pl.when(pl.program_id(1) == 0)
    def _():
        s_ref[...] = s0_ref[...]
        carry_ref[...] = sp_ref[0]

    p = prw_ref[0]
    row = _iota((c, 1), 0)
    prev = jnp.where(row == 0, carry_ref[...], pltpu.roll(p, shift=1, axis=0))
    carry_ref[...] = p[c - 1:c]
    rw = p + (prev - p) * mu_ref[...]
    r, k, v = rw[:, 0:256], rw[:, 256:512], rw[:, 512:768]
    tail = rw[:, 768:896]
    wd, ad, gd = tail[:, 0:32], tail[:, 32:64], tail[:, 64:128]
    w_log = -_softplus(-(w0_ref[...] + _mm(jnp.tanh(wd), w2_ref[...]))) - 0.5
    logw = -jnp.exp(w_log)
    a = _sigmoid(a0_ref[...] + _mm(ad, a2_ref[...]))
    g = _mm(_sigmoid(gd), g2_ref[...])
    if has_vres:
        v = v + (vf_ref[0] - v) * _sigmoid(v0_ref[...] + _mm(_mm(v, v1_ref[...]), v2_ref[...]))
    else:
        vout_ref[0] = v
    e64 = _block_ones(RW_W, RW_HD)
    kk = k * kk_ref[...]
    kk = kk * lax.rsqrt(_mm2(kk * kk, e64) + 1e-12)
    k2 = k * (1.0 + (a - 1.0) * ka_ref[...])
    bv = kk * a
    if t_valid < c:
        valid = row < t_valid
        logw = jnp.where(valid, logw, 0.0)
        kk = jnp.where(valid, kk, 0.0)
        bv = jnp.where(valid, bv, 0.0)
        k2 = jnp.where(valid, k2, 0.0)
        v = jnp.where(valid, v, 0.0)

    ti, si = _iota((c, c), 0), _iota((c, c), 1)
    g_incl = _mm3((si <= ti).astype(F32), logw)
    g_last = g_incl[c - 1:c]
    e_incl = jnp.exp(g_incl)
    e_neg = jnp.exp(-g_incl)
    e_last = jnp.exp(g_last - g_incl)
    kq = kk * jnp.exp(g_incl - logw)
    rq = r * e_incl
    bd, kd = bv * e_neg, k2 * e_neg
    bl, kl = bv * e_last, k2 * e_last
    s_decay = jnp.exp(g_last)

    ys = []
    for h in range(RW_HEADS):
        hs = slice(h * RW_HD, (h + 1) * RW_HD)
        s0 = s_ref[0, h]
        vh = v[:, hs]
        a_b = jnp.where(si < ti, _mm3(kq[:, hs], bd[:, hs], 1, 1), 0.0)
        a_k = jnp.where(si < ti, _mm3(kq[:, hs], kd[:, hs], 1, 1), 0.0)
        m_b = jnp.where(si <= ti, _mm3(rq[:, hs], bd[:, hs], 1, 1), 0.0)
        m_k = jnp.where(si <= ti, _mm3(rq[:, hs], kd[:, hs], 1, 1), 0.0)
        t_inv = _tri_inverse(a_b, c)
        u = -_mm3(t_inv, _mm3(kq[:, hs], s0, 1, 1) + _mm3(a_k, vh))
        yh = _mm3(rq[:, hs], s0, 1, 1) + _mm3(m_b, u) + _mm3(m_k, vh)
        s_ref[0, h] = (s0 * s_decay[:, hs] + _mm3(u, bl[:, hs], 0, 0) + _mm3(vh, kl[:, hs], 0, 0))
        mu_h = jnp.mean(yh, axis=-1, keepdims=True)
        var = jnp.mean(jnp.square(yh - mu_h), axis=-1, keepdims=True)
        yn = (yh - mu_h) * lax.rsqrt(var + RW_GN_EPS)
        bonus = jnp.sum(r[:, hs] * k2[:, hs] * rk_ref[:, hs], axis=-1, keepdims=True) * vh
        ys.append((yn, bonus))
    yn = jnp.concatenate([t[0] for t in ys], axis=1)
    bonus = jnp.concatenate([t[1] for t in ys], axis=1)
    y_ref[0] = (yn * lng_ref[...] + lnb_ref[...] + bonus) * g


def rwkv_mixer(prw, shift_proj, s0, v_first, p, t_valid):
    b, t, _ = prw.shape
    c = min(RW_CHUNK, t)
    has_vres = v_first is not None
    tok = lambda w: pl.BlockSpec((1, c, w), lambda i, j: (i, j, 0))
    per_b = lambda shape: pl.BlockSpec((1,) + shape, lambda i, j: (i,) + (0,) * len(shape))
    vec = lambda a: a.reshape(1, -1)
    ins = [prw, shift_proj, s0]
    specs = [tok(RW_COLS), per_b((1, RW_COLS)), per_b((RW_HEADS, RW_HD, RW_HD))]
    if has_vres:
        ins.append(v_first)
        specs.append(tok(RW_W))
    params = [vec(p['rw_mu']), vec(p['rw_w0']), p['rw_w2'], vec(p['rw_a0']), p['rw_a2'], p['rw_g2']]
    if has_vres:
        params += [vec(p['rw_v0']), p['rw_v1'], p['rw_v2']]
    params += [vec(p['rw_k_k']), vec(p['rw_k_a']), vec(p['rw_r_k']), vec(p['rw_ln_g']), vec(p['rw_ln_b'])]
    ins += params
    specs += [_full(a.shape) for a in params]
    y_shape = jax.ShapeDtypeStruct((b, t, RW_W), F32)
    s_shape = jax.ShapeDtypeStruct((b, RW_HEADS, RW_HD, RW_HD), F32)
    s_spec = per_b((RW_HEADS, RW_HD, RW_HD))
    if has_vres:
        out_shape, out_specs = [y_shape, s_shape], [tok(RW_W), s_spec]
    else:
        out_shape, out_specs = [y_shape, y_shape, s_shape], [tok(RW_W), tok(RW_W), s_spec]
    outs = pl.pallas_call(
        functools.partial(_rwkv_kernel, has_vres, c, t_valid),
        grid=(b, t // c),
        in_specs=specs,
        out_specs=out_specs,
        out_shape=out_shape,
        scratch_shapes=[pltpu.VMEM((1, RW_COLS), F32)],
        compiler_params=_cparams(("parallel", "arbitrary")),
    )(*ins)
    if has_vres:
        return outs[0], v_first, outs[1]
    return outs[0], outs[1], outs[2]


def _gla_kernel(c, sub, t_valid, pg_ref, s0_ref, a2_ref, ab_ref, ng_ref, y_ref, s_ref):
    @pl.when(pl.program_id(1) == 0)
    def _():
        s_ref[...] = s0_ref[...]

    expand = (_iota((GLA_KW, GLA_W), 0) // GLA_DK == _iota((GLA_KW, GLA_W), 1) // GLA_DV)
    expand_b = expand.astype(BF16)
    own_block = (_iota((GLA_W, GLA_KW), 0) // GLA_DV == _iota((GLA_W, GLA_KW), 1) // GLA_DK)
    e64 = _block_ones(GLA_W, GLA_DV)
    ti, si = _iota((sub, sub), 0), _iota((sub, sub), 1)
    tri = (si <= ti).astype(F32)
    row = _iota((sub, 1), 0)
    st = s_ref[0]
    for j in range(c // sub):
        pg = pg_ref[0, j * sub:(j + 1) * sub, :]
        q = pg[:, 0:128] * (GLA_DK ** -0.5)
        k = pg[:, 128:256]
        v = pg[:, 256:512]
        rg = pg[:, 512:768]
        ad = pg[:, 768:768 + GLA_A_LORA]
        z = _mm(ad, a2_ref[...]) + ab_ref[...]
        la = (jnp.minimum(z, 0.0) - jnp.log(1.0 + jnp.exp(-jnp.abs(z)))) / GLA_TAU
        if t_valid < c:
            valid = (row + j * sub) < t_valid
            la = jnp.where(valid, la, 0.0)
            k = jnp.where(valid, k, 0.0)
        bcum = _mm3(tri, la)
        b_last = bcum[sub - 1:sub]
        prods = []
        for s in range(sub):
            dec = jnp.exp(jnp.minimum(bcum - bcum[s:s + 1], 0.0))
            prods.append(jnp.where(row >= s, q * k[s:s + 1] * dec, 0.0))
        amat = _mm2(jnp.concatenate(prods, axis=0), expand_b)
        o = _mm3(q * jnp.exp(bcum), st, 1, 1)
        for s in range(sub):
            o = o + amat[s * sub:(s + 1) * sub] * v[s:s + 1]
        kdec = k * jnp.exp(b_last - bcum)
        upd = _mm3(v, kdec, 0, 0)
        st = st * jnp.exp(b_last) + jnp.where(own_block, upd, 0.0)
        ms = _mm2(o * o, e64) * (1.0 / GLA_DV)
        on = o * lax.rsqrt(ms + EPS) * ng_ref[...]
        y_ref[0, j * sub:(j + 1) * sub, :] = on * (rg * _sigmoid(rg))
    s_ref[0] = st


def gla_mixer(pg, s0t, p, t_valid):
    b, t, _ = pg.shape
    sub = min(GLA_SUB, t)
    c = min(4 * sub, t)
    tok = lambda w: pl.BlockSpec((1, c, w), lambda i, j: (i, j, 0))
    st_spec = pl.BlockSpec((1, GLA_W, GLA_KW), lambda i, j: (i, 0, 0))
    ng = jnp.tile(p['gla_norm_g'], GLA_HEADS).reshape(1, GLA_W)
    return pl.pallas_call(
        functools.partial(_gla_kernel, c, sub, t_valid),
        grid=(b, t // c),
        in_specs=[tok(GLA_PAD), st_spec, _full((GLA_A_LORA, GLA_KW)), _full((1, GLA_KW)), _full((1, GLA_W))],
        out_specs=[tok(GLA_W), st_spec],
        out_shape=[jax.ShapeDtypeStruct((b, t, GLA_W), F32), jax.ShapeDtypeStruct((b, GLA_W, GLA_KW), F32)],
        compiler_params=_cparams(("parallel", "arbitrary")),
    )(pg, s0t, p['gla_a2'], p['gla_ab'].reshape(1, GLA_KW), ng)


def _s5_kernel(rows, seg, carry_state, u_ref, hre_ref, him_ref, pre_ref, pim_ref, bre_ref, bim_ref,
               cre_ref, cim_ref, d_ref, ga_ref, gb_ref, y_ref, ore_ref, oim_ref, *scratch):
    u = u_ref[...]
    h_re = _mm3(u, bre_ref[...])
    h_im = _mm3(u, bim_ref[...])
    tpos = _iota((rows, 1), 0) % seg
    d = 1
    while d < seg:
        ar, ai = pre_ref[d - 1:d], pim_ref[d - 1:d]
        sr = jnp.where(tpos >= d, pltpu.roll(h_re, shift=d, axis=0), 0.0)
        sm = jnp.where(tpos >= d, pltpu.roll(h_im, shift=d, axis=0), 0.0)
        h_re, h_im = h_re + ar * sr - ai * sm, h_im + ar * sm + ai * sr
        d *= 2
    if carry_state:
        cre_s, cim_s = scratch

        @pl.when(pl.program_id(1) == 0)
        def _():
            cre_s[...] = hre_ref[0]
            cim_s[...] = him_ref[0]
        i_re, i_im = cre_s[...], cim_s[...]
    else:
        i_re, i_im = hre_ref[...], him_ref[...]
    pr, pi = pre_ref[...], pim_ref[...]
    h_re, h_im = h_re + pr * i_re - pi * i_im, h_im + pr * i_im + pi * i_re
    if carry_state:
        cre_s[...] = h_re[rows - 1:rows]
        cim_s[...] = h_im[rows - 1:rows]
        ore_ref[0] = h_re[rows - 1:rows]
        oim_ref[0] = h_im[rows - 1:rows]
    else:
        ore_ref[...] = h_re
        oim_ref[...] = h_im
    y = _mm3(h_re, cre_ref[...]) - _mm3(h_im, cim_ref[...]) + d_ref[...] * u
    z = 0.5 * y * (1.0 + jnp.tanh(0.7978845608028654 * (y + 0.044715 * y * y * y)))
    y_ref[...] = _mm(z, ga_ref[...]) * _sigmoid(_mm(z, gb_ref[...]))


def s5_mixer(u, h0_re, h0_im, p, seq_len):
    n = u.shape[0]
    b = n // seq_len
    f32 = F32
    a_re, a_im = p['s5_a_re'], p['s5_a_im']
    dt = jnp.exp(p['s5_log_dt'])[:, None]
    mag = jnp.exp(a_re * dt)
    ab_re, ab_im = mag * jnp.cos(a_im * dt), mag * jnp.sin(a_im * dt)
    den = a_re * a_re + a_im * a_im
    cr = ((ab_re - 1.0) * a_re + ab_im * a_im) / den
    ci = (ab_im * a_re - (ab_re - 1.0) * a_im) / den
    bb_re = cr[..., None] * p['s5_b_re'] - ci[..., None] * p['s5_b_im']
    bb_im = cr[..., None] * p['s5_b_im'] + ci[..., None] * p['s5_b_re']
    eye_g = jnp.eye(S5_GROUPS, dtype=f32)

    def in_blockdiag(w):
        return jnp.einsum('gpi,gh->gihp', w, eye_g).reshape(S5_W, S5_STATE)

    def out_blockdiag(w):
        return jnp.einsum('gip,gh->gphi', w, eye_g).reshape(S5_STATE, S5_W)

    carry_state = seq_len > S5_TILE
    rows = S5_TILE
    seg = rows if carry_state else seq_len
    pw_re, pw_im = ab_re.reshape(1, S5_STATE), ab_im.reshape(1, S5_STATE)
    while pw_re.shape[0] < seg:
        tr, ti_ = pw_re[-1:], pw_im[-1:]
        pw_re, pw_im = (jnp.concatenate([pw_re, pw_re * tr - pw_im * ti_], axis=0),
                        jnp.concatenate([pw_im, pw_re * ti_ + pw_im * tr], axis=0))
    pw_re, pw_im = jnp.tile(pw_re, (rows // seg, 1)), jnp.tile(pw_im, (rows // seg, 1))
    weights = [pw_re, pw_im, in_blockdiag(bb_re), in_blockdiag(bb_im),
               out_blockdiag(p['s5_c_re']), out_blockdiag(p['s5_c_im']),
               p['s5_d'].reshape(1, S5_W), p['s5_glu_a'], p['s5_glu_b']]
    wspecs = [_full(w.shape) for w in weights]
    if carry_state:
        tiles = seq_len // rows
        grid = (b, tiles)
        tok = lambda w: pl.BlockSpec((rows, w), lambda i, j: (i * tiles + j, 0))
        st_in = pl.BlockSpec((1, 1, S5_STATE), lambda i, j: (i, 0, 0))
        ins = [u, h0_re.reshape(b, 1, S5_STATE), h0_im.reshape(b, 1, S5_STATE)]
        st_shape = jax.ShapeDtypeStruct((b, 1, S5_STATE), f32)
        scratch = [pltpu.VMEM((1, S5_STATE), f32)] * 2
        sem = ("parallel", "arbitrary")
        st_out = st_in
    else:
        grid = (n // rows,)
        tok = lambda w: pl.BlockSpec((rows, w), lambda i: (i, 0))
        st_in = tok(S5_STATE)
        ins = [u, jnp.repeat(h0_re, seq_len, axis=0), jnp.repeat(h0_im, seq_len, axis=0)]
        st_shape = jax.ShapeDtypeStruct((n, S5_STATE), f32)
        scratch = []
        sem = ("parallel",)
        st_out = st_in
    y, o_re, o_im = pl.pallas_call(
        functools.partial(_s5_kernel, rows, seg, carry_state),
        grid=grid,
        in_specs=[tok(S5_W), st_in, st_in] + wspecs,
        out_specs=[tok(S5_W), st_out, st_out],
        out_shape=[jax.ShapeDtypeStruct((n, S5_W), f32), st_shape, st_shape],
        scratch_shapes=scratch,
        compiler_params=_cparams(sem),
    )(*ins, *weights)
    if carry_state:
        return y, o_re.reshape(b, S5_STATE), o_im.reshape(b, S5_STATE)
    last = lambda h: h.reshape(b, seq_len, S5_STATE)[:, -1]
    return y, last(o_re), last(o_im)


def _rope_rotate(x, cos_t, sin_a, sin_b):
    w = x.shape[1]
    return x * cos_t + pltpu.roll(x, shift=w - MLA_ROPE // 2, axis=1) * sin_a + pltpu.roll(x, shift=MLA_ROPE // 2, axis=1) * sin_b


def _mla_keys(c, kr, cos_t, sin_a, sin_b, wuk_ref, pk_ref):
    k_nope = _mm(c, wuk_ref[...])
    k_rope = _rope_rotate(_mm3(kr, pk_ref[...]), cos_t, sin_a, sin_b)
    return k_nope, k_rope


def _mla_prep_kernel(pm_ref, cos_ref, sina_ref, sinb_ref, qag_ref, wuq_ref, qg_ref, kvag_ref,
                     wuk_ref, wuv_ref, pk_ref, q_ref, k_ref, v_ref, c_ref, kr_ref):
    pm = pm_ref[...]
    qd, kvd, kr = pm[:, 0:256], pm[:, 256:512], pm[:, 512:512 + MLA_ROPE]
    tile4 = lambda t: jnp.concatenate([t] * MLA_HEADS, axis=1)
    cos_t, sin_a, sin_b = tile4(cos_ref[...]), tile4(sina_ref[...]), tile4(sinb_ref[...])
    e128 = _block_ones(MLA_HW, MLA_HP)
    qn = qd * lax.rsqrt(jnp.mean(qd * qd, axis=-1, keepdims=True) + EPS) * qag_ref[...]
    q = _mm(qn, wuq_ref[...])
    q = q * lax.rsqrt(_mm2(q * q, e128) * (1.0 / MLA_QK) + EPS) * qg_ref[...]
    q = _rope_rotate(q, cos_t, sin_a, sin_b)
    c = kvd * lax.rsqrt(jnp.mean(kvd * kvd, axis=-1, keepdims=True) + EPS) * kvag_ref[...]
    c_ref[...] = c
    kr_ref[...] = kr
    k_nope, k_rope = _mla_keys(c, kr, cos_t, sin_a, sin_b, wuk_ref, pk_ref)
    ssq = _mm2(k_nope * k_nope, e128) + jnp.sum(kr * kr, axis=-1, keepdims=True)
    kfull = (k_nope + k_rope) * (lax.rsqrt(ssq * (1.0 / MLA_QK) + EPS) * MLA_SCALE)
    vfull = _mm(c, wuv_ref[...])
    for h in range(MLA_HEADS):
        hs = slice(h * MLA_HP, (h + 1) * MLA_HP)
        q_ref[h] = _bf(q[:, hs])
        k_ref[h] = _bf(kfull[:, hs])
        v_ref[h] = _bf(vfull[:, hs])


def _flash_kernel(tq, tk, q_ref, k_ref, v_ref, o_ref):
    iq = pl.program_id(2)
    q = q_ref[0]
    q_pos = iq * tq + _iota((tq, tk), 0)

    def body(ik, carry):
        m, l, acc = carry
        off = pl.multiple_of(ik * tk, tk)
        kt = k_ref[0, pl.ds(off, tk), :]
        vt = v_ref[0, pl.ds(off, tk), :]
        s = _dg(q, kt, 1, 1)
        s = jnp.where(off + _iota((tq, tk), 1) <= q_pos, s, NEG_INF)
        m_new = jnp.maximum(m, jnp.max(s, axis=-1, keepdims=True))
        alpha = jnp.exp(m - m_new)
        pexp = jnp.exp(s - m_new)
        l = alpha * l + jnp.sum(pexp, axis=-1, keepdims=True)
        acc = alpha * acc + jnp.dot(_bf(pexp), vt, preferred_element_type=F32)
        return m_new, l, acc

    init = (jnp.full((tq, 1), NEG_INF, F32), jnp.zeros((tq, 1), F32), jnp.zeros((tq, MLA_HP), F32))
    n_k = (iq * tq + tq + tk - 1) // tk
    m, l, acc = lax.fori_loop(0, n_k, body, init)
    o_ref[...] = acc / l


def mla_prompt(pm, p, batch, seq_len, tables):
    n = pm.shape[0]
    rows = min(ROW_TILE, seq_len)
    tiles = seq_len // rows
    cos_t, sin_a, sin_b = tables
    tok = lambda w: pl.BlockSpec((rows, w), lambda i: (i, 0))
    tab = pl.BlockSpec((rows, MLA_HP), lambda i: (i % tiles, 0))
    hd = pl.BlockSpec((MLA_HEADS, rows, MLA_HP), lambda i: (0, i, 0))
    weights = [p['qag'], p['wuq_pad'], p['qg_pad'], p['kvag'], p['wuk_pad'], p['wuv_pad'], p['pk']]
    hshape = jax.ShapeDtypeStruct((MLA_HEADS, n, MLA_HP), BF16)
    q, k, v, c, kr = pl.pallas_call(
        _mla_prep_kernel,
        grid=(n // rows,),
        in_specs=[tok(MLA_PAD), tab, tab, tab] + [_full(w.shape) for w in weights],
        out_specs=[hd, hd, hd, tok(MLA_LORA), tok(MLA_ROPE)],
        out_shape=[hshape, hshape, hshape, jax.ShapeDtypeStruct((n, MLA_LORA), F32),
                   jax.ShapeDtypeStruct((n, MLA_ROPE), F32)],
        compiler_params=_cparams(("parallel",)),
    )(pm, cos_t, sin_a, sin_b, *weights)
    tq = min(ATT_TQ, seq_len)
    tk = min(ATT_TK, seq_len)
    nq = seq_len // tq
    kv_spec = pl.BlockSpec((1, seq_len, MLA_HP), lambda b, h, i: (h, b, 0))
    o = pl.pallas_call(
        functools.partial(_flash_kernel, tq, tk),
        grid=(batch, MLA_HEADS, nq),
        in_specs=[pl.BlockSpec((1, tq, MLA_HP), lambda b, h, i: (h, b * nq + i, 0)), kv_spec, kv_spec],
        out_specs=pl.BlockSpec((tq, MLA_HP), lambda b, h, i: (b * nq + i, h)),
        out_shape=jax.ShapeDtypeStruct((n, MLA_HW), F32),
        compiler_params=_cparams(("parallel", "parallel", "arbitrary")),
    )(q, k, v)
    return o, c, kr


def _mla_decode_kernel(n_pages, t_new, *refs):
    (pm_ref, cosn_ref, sinan_ref, sinbn_ref, cosp_ref, sinap_ref, sinbp_ref,
     qag_ref, wuq_ref, qg_ref, kvag_ref, wuk_ref, wuv_ref, pk_ref) = refs[1:15]
    c_refs = refs[15:15 + n_pages]
    kr_refs = refs[15 + n_pages:15 + 2 * n_pages]
    o_ref, c_out, kr_out, q_s, m_s, l_s, acc_s = refs[15 + 2 * n_pages:]
    j = pl.program_id(1)
    nq = MLA_HEADS * SUBLANE
    tile4 = lambda t: jnp.concatenate([t] * MLA_HEADS, axis=1)
    e128 = _block_ones(MLA_HW, MLA_HP)
    row_head = (_iota((nq, MLA_HW), 0) // SUBLANE == _iota((nq, MLA_HW), 1) // MLA_HP)
    head_sel = row_head.astype(BF16)
    ones_r = jnp.ones((SUBLANE, MLA_ROPE), BF16)

    def attend(c, kr, cos_t, sin_a, sin_b, mask):
        k_nope, k_rope = _mla_keys(c, kr, cos_t, sin_a, sin_b, wuk_ref, pk_ref)
        ssq = _mm2l(head_sel, k_nope * k_nope, 1, 1)
        ssq = ssq + _mm2l(ones_r, kr * kr, 1, 1)[0:1]
        s = _mm(q_s[...], k_nope + k_rope, 1, 1) * (lax.rsqrt(ssq * (1.0 / MLA_QK) + EPS) * MLA_SCALE)
        if mask is not None:
            s = jnp.where(mask, s, NEG_INF)
        m_old = m_s[...]
        m_new = jnp.maximum(m_old, jnp.max(s, axis=-1, keepdims=True))
        alpha = jnp.exp(m_old - m_new)
        pexp = jnp.exp(s - m_new)
        l_s[...] = alpha * l_s[...] + jnp.sum(pexp, axis=-1, keepdims=True)
        m_s[...] = m_new
        acc_s[...] = acc_s[...] * alpha + _mm(pexp, c)

    pm = pm_ref[0]
    qd, kvd, kr_new = pm[:, 0:256], pm[:, 256:512], pm[:, 512:512 + MLA_ROPE]
    c_new = kvd * lax.rsqrt(jnp.mean(kvd * kvd, axis=-1, keepdims=True) + EPS) * kvag_ref[...]
    cos_n, sin_an, sin_bn = tile4(cosn_ref[...]), tile4(sinan_ref[...]), tile4(sinbn_ref[...])

    @pl.when(j == 0)
    def _():
        qn = qd * lax.rsqrt(jnp.mean(qd * qd, axis=-1, keepdims=True) + EPS) * qag_ref[...]
        q = _mm(qn, wuq_ref[...])
        q = q * lax.rsqrt(_mm2(q * q, e128) * (1.0 / MLA_QK) + EPS) * qg_ref[...]
        q = _rope_rotate(q, cos_n, sin_an, sin_bn)
        q_s[...] = jnp.where(row_head, jnp.concatenate([q] * MLA_HEADS, axis=0), 0.0)
        m_s[...] = jnp.full(m_s.shape, NEG_INF, F32)
        l_s[...] = jnp.zeros(l_s.shape, F32)
        acc_s[...] = jnp.zeros(acc_s.shape, F32)
        c_out[0] = c_new
        kr_out[0] = kr_new

    for i in range(n_pages):
        rs = slice(i * PAGE_SIZE, (i + 1) * PAGE_SIZE)
        attend(c_refs[i][0, 0], kr_refs[i][0, 0], tile4(cosp_ref[rs, :]), tile4(sinap_ref[rs, :]),
               tile4(sinbp_ref[rs, :]), None)

    @pl.when(j == pl.num_programs(1) - 1)
    def _():
        qry_t = _iota((nq, SUBLANE), 0) % SUBLANE
        key_t = _iota((nq, SUBLANE), 1)
        attend(c_new, kr_new, cos_n, sin_an, sin_bn, (key_t <= qry_t) & (key_t < t_new))
        o_lat = acc_s[...] / l_s[...]
        full = jnp.where(row_head, _mm(o_lat, wuv_ref[...]), 0.0)
        gather = (_iota((SUBLANE, nq), 0) == _iota((SUBLANE, nq), 1) % SUBLANE).astype(BF16)
        o_ref[0] = _mm2l(gather, full)


def mla_decode(pm, cache_c, cache_kr, page_table, layer, p, tables_new, tables_past, t_new):
    b = pm.shape[0]
    pages = page_table.shape[1]
    steps = pages // DEC_PAGES
    per_b = lambda w: pl.BlockSpec((1, SUBLANE, w), lambda i, j, pt: (i, 0, 0))
    tab_new = pl.BlockSpec((SUBLANE, MLA_HP), lambda i, j, pt: (0, 0))
    tab_past = pl.BlockSpec((DEC_PAGES * PAGE_SIZE, MLA_HP), lambda i, j, pt: (j, 0))
    weights = [p['qag'], p['wuq_pad'], p['qg_pad'], p['kvag'], p['wuk_pad'], p['wuv_lat'], p['pk']]
    wspecs = [pl.BlockSpec(w.shape, lambda i, j, pt, nd=w.ndim: (0,) * nd) for w in weights]

    def page_spec(w, slot):
        return pl.BlockSpec((1, 1, PAGE_SIZE, w),
                            lambda i, j, pt: (layer, pt[i, j * DEC_PAGES + slot], 0, 0))

    c_specs = [page_spec(MLA_LORA, s) for s in range(DEC_PAGES)]
    kr_specs = [page_spec(MLA_ROPE, s) for s in range(DEC_PAGES)]
    nq = MLA_HEADS * SUBLANE
    grid_spec = pltpu.PrefetchScalarGridSpec(
        num_scalar_prefetch=1,
        grid=(b, steps),
        in_specs=[per_b(MLA_PAD), tab_new, tab_new, tab_new, tab_past, tab_past, tab_past]
        + wspecs + c_specs + kr_specs,
        out_specs=[per_b(MLA_HW), per_b(MLA_LORA), per_b(MLA_ROPE)],
        scratch_shapes=[pltpu.VMEM((nq, MLA_HW), F32), pltpu.VMEM((nq, 1), F32),
                        pltpu.VMEM((nq, 1), F32), pltpu.VMEM((nq, MLA_LORA), F32)],
    )
    o, c, kr = pl.pallas_call(
        functools.partial(_mla_decode_kernel, DEC_PAGES, t_new),
        grid_spec=grid_spec,
        out_shape=[jax.ShapeDtypeStruct((b, SUBLANE, MLA_HW), F32),
                   jax.ShapeDtypeStruct((b, SUBLANE, MLA_LORA), F32),
                   jax.ShapeDtypeStruct((b, SUBLANE, MLA_ROPE), F32)],
        compiler_params=_cparams(("parallel", "arbitrary")),
    )(page_table, pm, *tables_new, *tables_past, *weights,
      *([cache_c] * DEC_PAGES), *([cache_kr] * DEC_PAGES))
    return o, c, kr


def rope_tables(pos):
    half = MLA_ROPE // 2
    inv = ROPE_THETA ** (-jnp.arange(half, dtype=F32) / half)
    ang = pos.astype(F32)[:, None] * inv
    cos, sin = jnp.cos(ang), jnp.sin(ang)
    n = pos.shape[0]
    ones, zeros = jnp.ones((n, MLA_NOPE), F32), jnp.zeros((n, MLA_NOPE), F32)
    zpad = jnp.zeros((n, MLA_HP - MLA_QK), F32)
    zh = jnp.zeros((n, half), F32)
    cos_t = jnp.concatenate([ones, cos, cos, zpad + 1.0], axis=1)
    sin_a = jnp.concatenate([zeros, -sin, zh, zpad], axis=1)
    sin_b = jnp.concatenate([zeros, zh, sin, zpad], axis=1)
    return cos_t, sin_a, sin_b


def _merge_kernel(x_ref, xn_ref, yrw_ref, ygla_ref, ys5_ref, ymla_ref, wg_ref, wb_ref, wbm_ref, wo_ref, o_ref):
    xb = _bf(xn_ref[...])
    merged = None
    for n, y_ref in enumerate((yrw_ref, ygla_ref, ys5_ref, ymla_ref)):
        gate = _sigmoid(jnp.dot(xb, wg_ref[:, n * D_MODEL:(n + 1) * D_MODEL], preferred_element_type=F32))
        w = wbm_ref[...] if n == 3 else wb_ref[n]
        term = gate * jnp.dot(_bf(y_ref[...]), w, preferred_element_type=F32)
        merged = term if merged is None else merged + term
    o_ref[...] = x_ref[...] + jnp.dot(_bf(merged), wo_ref[...], preferred_element_type=F32)


def merge(x, xn, y_rw, y_gla, y_s5, y_mla, p):
    n = x.shape[0]
    row = lambda w: pl.BlockSpec((ROW_TILE, w), lambda i: (i, 0))
    weights = [p['w_gate'], p['w_branch3'], p['w_branch_mla'], p['w_out']]
    return pl.pallas_call(
        _merge_kernel,
        grid=(n // ROW_TILE,),
        in_specs=[row(D_MODEL), row(D_MODEL), row(BRANCH_W), row(BRANCH_W), row(BRANCH_W), row(MLA_HW)]
        + [_full(w.shape) for w in weights],
        out_specs=row(D_MODEL),
        out_shape=jax.ShapeDtypeStruct((n, D_MODEL), F32),
        compiler_params=_cparams(("parallel",)),
    )(x, xn, y_rw, y_gla, y_s5, y_mla, *weights)


def _mlp_kernel(ff_chunk, x_ref, g_ref, w1_ref, w2_ref, o_ref):
    x = x_ref[...]
    xb = _bf(x * lax.rsqrt(jnp.mean(x * x, axis=-1, keepdims=True) + EPS) * g_ref[...])
    acc = x
    for o in range(0, D_FF, ff_chunk):
        h = jnp.maximum(jnp.dot(xb, w1_ref[:, o:o + ff_chunk], preferred_element_type=F32), 0.0)
        acc = acc + jnp.dot(_bf(h * h), w2_ref[o:o + ff_chunk, :], preferred_element_type=F32)
    o_ref[...] = acc


def mlp(x, p):
    n = x.shape[0]
    row = pl.BlockSpec((ROW_TILE, D_MODEL), lambda i: (i, 0))
    return pl.pallas_call(
        functools.partial(_mlp_kernel, 1024),
        grid=(n // ROW_TILE,),
        in_specs=[row, _full((1, D_MODEL)), _full((D_MODEL, D_FF)), _full((D_FF, D_MODEL))],
        out_specs=row,
        out_shape=jax.ShapeDtypeStruct((n, D_MODEL), F32),
        compiler_params=_cparams(("parallel",)),
    )(x, p['norm2_g'].reshape(1, D_MODEL), p['w_ff1'], p['w_ff2'])


def _pad_cols(w, width):
    return jnp.pad(w, ((0, 0), (0, width - w.shape[1])))


def _head_pad(w, used):
    kdim = w.shape[0]
    w = w.reshape(kdim, MLA_HEADS, used)
    return jnp.pad(w, ((0, 0), (0, 0), (0, MLA_HP - used))).reshape(kdim, MLA_HW)


def prepare_layer(l, W):
    p = {}
    w_in = W['w_in'][l]
    o_gla, o_s5, o_mla = RW_COLS, RW_COLS + GLA_COLS, RW_COLS + GLA_COLS + S5_W
    w_gla = w_in[:, o_gla:o_s5]
    w_gla = jnp.concatenate([w_gla[:, 0:512], w_gla[:, 528:784], w_gla[:, 512:528]], axis=1)
    p['w_rw'] = _bf(w_in[:, :RW_COLS])
    p['w_all'] = _bf(jnp.concatenate([w_in[:, :RW_COLS], _pad_cols(w_gla, GLA_PAD), w_in[:, o_s5:o_mla],
                                      _pad_cols(w_in[:, o_mla:], MLA_PAD)], axis=1))
    for name in ('norm1_g', 'rw_mu', 'rw_w0', 'rw_w2', 'rw_a0', 'rw_a2', 'rw_g2', 'rw_k_k', 'rw_k_a',
                 'rw_ln_g', 'rw_ln_b', 'gla_a2', 'gla_ab', 'gla_norm_g', 's5_a_re', 's5_a_im', 's5_log_dt',
                 's5_b_re', 's5_b_im', 's5_c_re', 's5_c_im', 's5_glu_a', 's5_glu_b', 'norm2_g'):
        p[name] = W[name][l]
    p['rw_r_k'] = W['rw_r_k'][l].reshape(RW_W)
    p['s5_d'] = W['s5_d'][l].reshape(S5_W)
    if l > 0:
        for name in ('rw_v0', 'rw_v1', 'rw_v2'):
            p[name] = W[name][l - 1]
    k_g = W['mla_k_g'][l]
    p['qag'] = W['mla_qa_g'][l].reshape(1, MLA_LORA)
    p['kvag'] = W['mla_kva_g'][l].reshape(1, MLA_LORA)
    p['wuq_pad'] = _head_pad(W['mla_w_uq'][l], MLA_QK)
    qg = W['mla_q_g'][l] * jnp.concatenate([k_g[:MLA_NOPE], jnp.ones((MLA_ROPE,), F32)])
    p['qg_pad'] = jnp.tile(jnp.pad(qg, (0, MLA_HP - MLA_QK)), MLA_HEADS).reshape(1, MLA_HW)
    p['wuk_pad'] = _head_pad(W['mla_w_uk'][l].reshape(MLA_LORA, MLA_HEADS * MLA_NOPE), MLA_NOPE)
    wuv = W['mla_w_uv'][l].reshape(MLA_LORA, MLA_HEADS * MLA_V)
    p['wuv_pad'] = _head_pad(wuv, MLA_V)
    p['wuv_lat'] = p['wuv_pad']
    place = jnp.pad(jnp.diag(k_g[MLA_NOPE:]), ((0, 0), (MLA_NOPE, MLA_HP - MLA_QK)))
    p['pk'] = jnp.tile(place, (1, MLA_HEADS))
    p['w_gate'] = _bf(W['w_gate'][l])
    wb = W['w_branch'][l]
    p['w_branch3'] = _bf(wb)
    wbm = wb[3].reshape(MLA_HEADS, MLA_V, D_MODEL)
    p['w_branch_mla'] = _bf(jnp.pad(wbm, ((0, 0), (0, MLA_HP - MLA_V), (0, 0))).reshape(MLA_HW, D_MODEL))
    p['w_out'] = _bf(W['w_out'][l])
    p['w_ff1'] = _bf(W['w_ff1'][l])
    p['w_ff2'] = _bf(W['w_ff2'][l])
    return p


def _gla_state_in(s):
    eye = jnp.eye(GLA_HEADS, dtype=F32)
    return jnp.einsum('bhdv,hg->bhvgd', s, eye).reshape(s.shape[0], GLA_W, GLA_KW)


def _gla_state_out(st):
    b = st.shape[0]
    st = st.reshape(b, GLA_HEADS, GLA_DV, GLA_HEADS, GLA_DK)
    return jnp.stack([st[:, h, :, h, :] for h in range(GLA_HEADS)], axis=1).swapaxes(2, 3)


def kernel(x_prompt, x_sample, cache_mla_latent, cache_mla_krope, page_table, state_rwkv, state_shift, state_gla, state_s5_re, state_s5_im, norm1_g, w_in, rw_mu, rw_w0, rw_w2, rw_a0, rw_a2, rw_g2, rw_v0, rw_v1, rw_v2, rw_k_k, rw_k_a, rw_r_k, rw_ln_g, rw_ln_b, gla_a2, gla_ab, gla_norm_g, s5_a_re, s5_a_im, s5_log_dt, s5_b_re, s5_b_im, s5_c_re, s5_c_im, s5_d, s5_glu_a, s5_glu_b, mla_qa_g, mla_w_uq, mla_kva_g, mla_w_uk, mla_w_uv, mla_q_g, mla_k_g, w_gate, w_branch, w_out, norm2_g, w_ff1, w_ff2):
    W = dict(norm1_g=norm1_g, w_in=w_in, rw_mu=rw_mu, rw_w0=rw_w0, rw_w2=rw_w2, rw_a0=rw_a0, rw_a2=rw_a2,
             rw_g2=rw_g2, rw_v0=rw_v0, rw_v1=rw_v1, rw_v2=rw_v2, rw_k_k=rw_k_k, rw_k_a=rw_k_a, rw_r_k=rw_r_k,
             rw_ln_g=rw_ln_g, rw_ln_b=rw_ln_b, gla_a2=gla_a2, gla_ab=gla_ab, gla_norm_g=gla_norm_g,
             s5_a_re=s5_a_re, s5_a_im=s5_a_im, s5_log_dt=s5_log_dt, s5_b_re=s5_b_re, s5_b_im=s5_b_im,
             s5_c_re=s5_c_re, s5_c_im=s5_c_im, s5_d=s5_d, s5_glu_a=s5_glu_a, s5_glu_b=s5_glu_b,
             mla_qa_g=mla_qa_g, mla_w_uq=mla_w_uq, mla_kva_g=mla_kva_g, mla_w_uk=mla_w_uk, mla_w_uv=mla_w_uv,
             mla_q_g=mla_q_g, mla_k_g=mla_k_g, w_gate=w_gate, w_branch=w_branch, w_out=w_out,
             norm2_g=norm2_g, w_ff1=w_ff1, w_ff2=w_ff2)
    bp, tp, _ = x_prompt.shape
    bs, ts, _ = x_sample.shape
    depth = w_in.shape[0]
    np_, ns = bp * tp, bs * ts
    past_len = page_table.shape[1] * PAGE_SIZE
    tpad = SUBLANE

    tables_p = rope_tables(jnp.arange(tp, dtype=jnp.int32))
    tables_new = rope_tables(past_len + jnp.arange(tpad, dtype=jnp.int32))
    tables_past = rope_tables(jnp.arange(past_len, dtype=jnp.int32))

    x = jnp.concatenate([x_prompt.reshape(np_, D_MODEL), x_sample.reshape(ns, D_MODEL)], axis=0)
    pad_t = lambda a: jnp.pad(a.reshape(bs, ts, a.shape[-1]), ((0, 0), (0, tpad - ts), (0, 0)))
    unpad = lambda a: a[:, :ts].reshape(ns, a.shape[-1])
    vf_p = vf_s = None
    outs = {k: [] for k in ('c_p', 'kr_p', 'c_s', 'kr_s', 'rw_p', 'rw_s', 'sh_p', 'sh_s', 'gla_p', 'gla_s',
                            're_p', 'im_p', 're_s', 'im_s')}
    for l in range(depth):
        p = prepare_layer(l, W)
        xn, prw, pgla, ps5, pmla = norm_proj(x, p['norm1_g'], p['w_all'])
        xn_s = xn[np_:].reshape(bs, ts, D_MODEL)
        outs['sh_p'].append(xn[:np_].reshape(bp, tp, D_MODEL)[:, -1])
        outs['sh_s'].append(xn_s[:, -1])

        y_rw_p, vf_p, s_rw_p = rwkv_mixer(prw[:np_].reshape(bp, tp, RW_COLS), jnp.zeros((bp, 1, RW_COLS), F32),
                                          jnp.zeros((bp, RW_HEADS, RW_HD, RW_HD), F32), vf_p, p, tp)
        shift_proj = small_matmul(state_shift[l], p['w_rw']).reshape(bs, 1, RW_COLS)
        y_rw_s, vf_s, s_rw_s = rwkv_mixer(pad_t(prw[np_:]), shift_proj, state_rwkv[l], vf_s, p, ts)
        outs['rw_p'].append(s_rw_p)
        outs['rw_s'].append(s_rw_s)
        y_rw = jnp.concatenate([y_rw_p.reshape(np_, RW_W), unpad(y_rw_s)], axis=0)

        y_gla_p, s_gla_p = gla_mixer(pgla[:np_].reshape(bp, tp, GLA_PAD),
                                     jnp.zeros((bp, GLA_W, GLA_KW), F32), p, tp)
        y_gla_s, s_gla_s = gla_mixer(pad_t(pgla[np_:]), _gla_state_in(state_gla[l]), p, ts)
        outs['gla_p'].append(_gla_state_out(s_gla_p))
        outs['gla_s'].append(_gla_state_out(s_gla_s))
        y_gla = jnp.concatenate([y_gla_p.reshape(np_, GLA_W), unpad(y_gla_s)], axis=0)

        zst = jnp.zeros((bp, S5_STATE), F32)
        y_s5_p, re_p, im_p = s5_mixer(ps5[:np_], zst, zst, p, tp)
        y_s5_s, re_s, im_s = s5_mixer(ps5[np_:], state_s5_re[l].reshape(bs, S5_STATE),
                                      state_s5_im[l].reshape(bs, S5_STATE), p, ts)
        for key, val, bb in (('re_p', re_p, bp), ('im_p', im_p, bp), ('re_s', re_s, bs), ('im_s', im_s, bs)):
            outs[key].append(val.reshape(bb, S5_GROUPS, S5_P))
        y_s5 = jnp.concatenate([y_s5_p, y_s5_s], axis=0)

        y_mla_p, c_p, kr_p = mla_prompt(pmla[:np_], p, bp, tp, tables_p)
        y_mla_s, c_s, kr_s = mla_decode(pad_t(pmla[np_:]), cache_mla_latent, cache_mla_krope, page_table, l, p,
                                        tables_new, tables_past, ts)
        outs['c_p'].append(c_p.reshape(bp, tp, MLA_LORA))
        outs['kr_p'].append(kr_p.reshape(bp, tp, MLA_ROPE))
        outs['c_s'].append(c_s[:, :ts])
        outs['kr_s'].append(kr_s[:, :ts])
        y_mla = jnp.concatenate([y_mla_p, unpad(y_mla_s)], axis=0)

        x = merge(x, xn, y_rw, y_gla, y_s5, y_mla, p)
        x = mlp(x, p)

    st = lambda k: jnp.stack(outs[k], axis=0)
    return (x[:np_].reshape(bp, tp, D_MODEL), x[np_:].reshape(bs, ts, D_MODEL),
            st('c_p'), st('kr_p'), st('c_s'), st('kr_s'),
            st('rw_p'), st('rw_s'), st('sh_p'), st('sh_s'),
            st('gla_p'), st('gla_s'), st('re_p'), st('im_p'), st('re_s'), st('im_s'))
```

```python
import functools

import jax
import jax.numpy as jnp
from jax import lax
from jax.experimental import pallas as pl
from jax.experimental.pallas import tpu as pltpu

F32 = jnp.float32
BF16 = jnp.bfloat16

D_MODEL = 1024
N_BRANCH = 4
BRANCH_W = 256
EPS = 1e-6
PAGE_SIZE = 128
RW_HEADS, RW_HD, RW_W = 4, 64, 256
RW_COLS = 896
RW_GN_EPS = 64e-5
GLA_HEADS, GLA_DK, GLA_DV, GLA_W = 4, 32, 64, 256
GLA_KW = GLA_HEADS * GLA_DK
GLA_A_LORA = 16
GLA_TAU = 16.0
GLA_COLS = 784
GLA_PAD = 896
S5_GROUP, S5_GROUPS, S5_P, S5_W = 16, 16, 64, 256
S5_STATE = S5_GROUPS * S5_P
MLA_HEADS, MLA_NOPE, MLA_ROPE, MLA_QK, MLA_V = 4, 64, 32, 96, 64
MLA_LORA = 256
MLA_COLS = 544
MLA_PAD = 640
MLA_HP = 128
MLA_HW = MLA_HEADS * MLA_HP
MLA_SCALE = MLA_QK ** -0.5
ROPE_THETA = 10000.0
NEG_INF = -1e30
D_FF = 4096
IN_PAD = RW_COLS + GLA_PAD + S5_W + MLA_PAD

V7X_VMEM_BYTES = 64 * 1024 * 1024
VMEM_LIMIT = V7X_VMEM_BYTES - 8 * 1024 * 1024
LANE = 128
SUBLANE = 8

ROW_TILE = 512
RW_CHUNK = 64
RW_BATCH = 4
GLA_SUB = 16
GLA_BATCH = 4
S5_TILE = 128
ATT_T = 512
DEC_PAGES = 16
DEC_SUB = 512


def _cparams(sem):
    return pltpu.CompilerParams(dimension_semantics=sem, vmem_limit_bytes=VMEM_LIMIT)


def _bf(x):
    return x.astype(BF16)


def _dg(a, b, ca, cb):
    return lax.dot_general(a, b, (((ca,), (cb,)), ((), ())), preferred_element_type=F32)


def _mm(a, b, ca=1, cb=0):
    return _dg(_bf(a), _bf(b), ca, cb)


def _split(x):
    hi = _bf(x)
    return hi, _bf(x - hi.astype(F32))


def _mm3(a, b, ca=1, cb=0):
    ah, al = _split(a)
    bh, bl = _split(b)
    return _dg(ah, bh, ca, cb) + _dg(al, bh, ca, cb) + _dg(ah, bl, ca, cb)


def _mm2(a, b01, ca=1, cb=0):
    ah, al = _split(a)
    return _dg(ah, b01, ca, cb) + _dg(al, b01, ca, cb)


def _mm2l(a01, b, ca=1, cb=0):
    bh, bl = _split(b)
    return _dg(a01, bh, ca, cb) + _dg(a01, bl, ca, cb)


def _sigmoid(x):
    return 1.0 / (1.0 + jnp.exp(-x))


def _softplus(x):
    return jnp.maximum(x, 0.0) + jnp.log(1.0 + jnp.exp(-jnp.abs(x)))


def _iota(shape, dim):
    return lax.broadcasted_iota(jnp.int32, shape, dim)


def _block_ones(n, blk, dtype=BF16):
    return (_iota((n, n), 0) // blk == _iota((n, n), 1) // blk).astype(dtype)


def _full(shape):
    nd = len(shape)
    return pl.BlockSpec(shape, lambda *_: (0,) * nd)


def _norm_proj_kernel(x_ref, g_ref, w_ref, xn_ref, rw_ref, gla_ref, s5_ref, mla_ref):
    x = x_ref[...]
    xn = x * lax.rsqrt(jnp.mean(x * x, axis=-1, keepdims=True) + EPS) * g_ref[...]
    xn_ref[...] = xn
    xb = _bf(xn)
    o = 0
    for ref in (rw_ref, gla_ref, s5_ref, mla_ref):
        w = ref.shape[1]
        ref[...] = jnp.dot(xb, w_ref[:, o:o + w], preferred_element_type=F32)
        o += w


def norm_proj(x, g, w_all):
    n = x.shape[0]
    widths = (RW_COLS, GLA_PAD, S5_W, MLA_PAD)
    row = lambda w: pl.BlockSpec((ROW_TILE, w), lambda i: (i, 0))
    return pl.pallas_call(
        _norm_proj_kernel,
        grid=(n // ROW_TILE,),
        in_specs=[row(D_MODEL), _full((1, D_MODEL)), _full((D_MODEL, IN_PAD))],
        out_specs=[row(D_MODEL)] + [row(w) for w in widths],
        out_shape=[jax.ShapeDtypeStruct((n, D_MODEL), F32)]
        + [jax.ShapeDtypeStruct((n, w), F32) for w in widths],
        compiler_params=_cparams(("parallel",)),
    )(x, g.reshape(1, D_MODEL), w_all)


def _matmul_kernel(a_ref, b_ref, o_ref):
    o_ref[...] = _mm(a_ref[...], b_ref[...])


def small_matmul(a, b):
    m, n = a.shape[0], b.shape[1]
    return pl.pallas_call(
        _matmul_kernel,
        out_shape=jax.ShapeDtypeStruct((m, n), F32),
        compiler_params=_cparams(None),
    )(a, b)


def _rwkv_kernel(has_vres, nb, c, t_valid, *refs):
    if has_vres:
        (prw_ref, sp_ref, s0_ref, vf_ref, mu_ref, w0_ref, w2_ref, a0_ref, a2_ref, g2_ref,
         v0_ref, v1_ref, v2_ref, kk_ref, ka_ref, rk_ref, lng_ref, lnb_ref,
         y_ref, s_ref, carry_ref) = refs
    else:
        (prw_ref, sp_ref, s0_ref, mu_ref, w0_ref, w2_ref, a0_ref, a2_ref, g2_ref,
         kk_ref, ka_ref, rk_ref, lng_ref, lnb_ref,
         y_ref, vout_ref, s_ref, carry_ref) = refs

    @pl.when(pl.program_id(1) == 0)
    def _():
        s_ref[...] = s0_ref[...]
        carry_ref[...] = sp_ref[...]

    n = nb * c
    rowc = _iota((c, 1), 0)
    rown = _iota((n, 1), 0)
    ti, si = _iota((c, c), 0), _iota((c, c), 1)
    tn, sn = _iota((n, n), 0), _iota((n, n), 1)
    tri = ((sn <= tn) & (sn // c == tn // c)).astype(BF16)
    e64 = _block_ones(RW_W, RW_HD)
    rows = lambda fn: jnp.concatenate([fn(bi) for bi in range(nb)], axis=0)

    p = rows(lambda bi: prw_ref[bi])
    prev = rows(lambda bi: jnp.where(rowc == 0, carry_ref[bi], pltpu.roll(prw_ref[bi], shift=1, axis=0)))
    for bi in range(nb):
        carry_ref[bi] = prw_ref[bi, c - 1:c, :]
    rw = p + (prev - p) * mu_ref[...]
    r, k, v = rw[:, 0:256], rw[:, 256:512], rw[:, 512:768]
    tail = rw[:, 768:896]
    wd, ad, gd = tail[:, 0:32], tail[:, 32:64], tail[:, 64:128]
    w_log = -_softplus(-(w0_ref[...] + _mm(jnp.tanh(wd), w2_ref[...]))) - 0.5
    logw = -jnp.exp(w_log)
    a = _sigmoid(a0_ref[...] + _mm(ad, a2_ref[...]))
    g = _mm(_sigmoid(gd), g2_ref[...])
    if has_vres:
        vf = rows(lambda bi: vf_ref[bi])
        v = v + (vf - v) * _sigmoid(v0_ref[...] + _mm(_mm(v, v1_ref[...]), v2_ref[...]))
    else:
        for bi in range(nb):
            vout_ref[bi] = v[bi * c:(bi + 1) * c]
    kk = k * kk_ref[...]
    kk = kk * lax.rsqrt(_mm2(kk * kk, e64) + 1e-12)
    k2 = k * (1.0 + (a - 1.0) * ka_ref[...])
    bv = kk * a
    if t_valid < c:
        valid = rown % c < t_valid
        logw = jnp.where(valid, logw, 0.0)
        kk = jnp.where(valid, kk, 0.0)
        bv = jnp.where(valid, bv, 0.0)
        k2 = jnp.where(valid, k2, 0.0)
        v = jnp.where(valid, v, 0.0)
    g_incl = _mm2l(tri, logw)
    g_last = rows(lambda bi: jnp.broadcast_to(g_incl[(bi + 1) * c - 1:(bi + 1) * c], (c, RW_W)))
    e_neg = jnp.exp(-g_incl)
    e_last = jnp.exp(g_last - g_incl)
    kq = kk * jnp.exp(g_incl - logw)
    rq = r * jnp.exp(g_incl)
    bd, kd = bv * e_neg, k2 * e_neg
    bl, kl = bv * e_last, k2 * e_last
    s_decay = jnp.exp(g_last)

    chains = [(bi, h) for bi in range(nb) for h in range(RW_HEADS)]
    cut = lambda t: [t[bi * c:(bi + 1) * c, h * RW_HD:(h + 1) * RW_HD] for bi, h in chains]
    kq_c, rq_c, bd_c, kd_c, bl_c, kl_c, v_c = cut(kq), cut(rq), cut(bd), cut(kd), cut(bl), cut(kl), cut(v)
    s0_c = [s_ref[bi, h] for bi, h in chains]
    a_b = [jnp.where(si < ti, _mm3(x, y, 1, 1), 0.0) for x, y in zip(kq_c, bd_c)]
    a_k = [jnp.where(si < ti, _mm(x, y, 1, 1), 0.0) for x, y in zip(kq_c, kd_c)]
    m_b = [jnp.where(si <= ti, _mm(x, y, 1, 1), 0.0) for x, y in zip(rq_c, bd_c)]
    m_k = [jnp.where(si <= ti, _mm(x, y, 1, 1), 0.0) for x, y in zip(rq_c, kd_c)]
    ks0 = [_mm(x, s, 1, 1) for x, s in zip(kq_c, s0_c)]
    rs0 = [_mm(x, s, 1, 1) for x, s in zip(rq_c, s0_c)]
    akv = [_mm(x, y) for x, y in zip(a_k, v_c)]
    mkv = [_mm(x, y) for x, y in zip(m_k, v_c)]
    eye = (si == ti).astype(F32)
    t_inv = [eye - x for x in a_b]
    pw = a_b
    lvl = 2
    while lvl < c:
        pw = [_mm3(x, x) for x in pw]
        t_inv = [x + _mm3(x, y) for x, y in zip(t_inv, pw)]
        lvl *= 2
    u = [-_mm(x, y + z) for x, y, z in zip(t_inv, ks0, akv)]
    yh = [x + _mm(mb, uu) + y for x, mb, uu, y in zip(rs0, m_b, u, mkv)]
    upd = [_mm(uu, x, 0, 0) + _mm(vv, y, 0, 0) for uu, x, vv, y in zip(u, bl_c, v_c, kl_c)]
    for (bi, h), s0, dd in zip(chains, s0_c, upd):
        s_ref[bi, h] = s0 * s_decay[(bi + 1) * c - 1:(bi + 1) * c, h * RW_HD:(h + 1) * RW_HD] + dd

    y = rows(lambda bi: jnp.concatenate(yh[bi * RW_HEADS:(bi + 1) * RW_HEADS], axis=1))
    mean = _mm2(y, e64) * (1.0 / RW_HD)
    yc = y - mean
    var = _mm2(yc * yc, e64) * (1.0 / RW_HD)
    bonus = _mm2(r * k2 * rk_ref[...], e64) * v
    out = (yc * lax.rsqrt(var + RW_GN_EPS) * lng_ref[...] + lnb_ref[...] + bonus) * g
    for bi in range(nb):
        y_ref[bi] = out[bi * c:(bi + 1) * c]


def rwkv_mixer(prw, shift_proj, s0, v_first, p, t_valid):
    b, t, _ = prw.shape
    c = min(RW_CHUNK, t)
    nb = min(RW_BATCH * max(1, RW_CHUNK // (4 * c)), b)
    has_vres = v_first is not None
    tok = lambda w: pl.BlockSpec((nb, c, w), lambda i, j: (i, j, 0))
    per_b = lambda shape: pl.BlockSpec((nb,) + shape, lambda i, j: (i,) + (0,) * len(shape))
    vec = lambda a: a.reshape(1, -1)
    ins = [prw, shift_proj, s0]
    specs = [tok(RW_COLS), per_b((1, RW_COLS)), per_b((RW_HEADS, RW_HD, RW_HD))]
    if has_vres:
        ins.append(v_first)
        specs.append(tok(RW_W))
    params = [vec(p['rw_mu']), vec(p['rw_w0']), p['rw_w2'], vec(p['rw_a0']), p['rw_a2'], p['rw_g2']]
    if has_vres:
        params += [vec(p['rw_v0']), p['rw_v1'], p['rw_v2']]
    params += [vec(p['rw_k_k']), vec(p['rw_k_a']), vec(p['rw_r_k']), vec(p['rw_ln_g']), vec(p['rw_ln_b'])]
    ins += params
    specs += [_full(a.shape) for a in params]
    y_shape = jax.ShapeDtypeStruct((b, t, RW_W), F32)
    s_shape = jax.ShapeDtypeStruct((b, RW_HEADS, RW_HD, RW_HD), F32)
    s_spec = per_b((RW_HEADS, RW_HD, RW_HD))
    if has_vres:
        out_shape, out_specs = [y_shape, s_shape], [tok(RW_W), s_spec]
    else:
        out_shape, out_specs = [y_shape, y_shape, s_shape], [tok(RW_W), tok(RW_W), s_spec]
    outs = pl.pallas_call(
        functools.partial(_rwkv_kernel, has_vres, nb, c, t_valid),
        grid=(b // nb, t // c),
        in_specs=specs,
        out_specs=out_specs,
        out_shape=out_shape,
        scratch_shapes=[pltpu.VMEM((nb, 1, RW_COLS), F32)],
        compiler_params=_cparams(("parallel", "arbitrary")),
    )(*ins)
    if has_vres:
        return outs[0], v_first, outs[1]
    return outs[0], outs[1], outs[2]


def _gla_kernel(nb, c, sub, t_valid, pg_ref, s0_ref, a2_ref, ab_ref, ng_ref, y_ref, s_ref):
    @pl.when(pl.program_id(1) == 0)
    def _():
        s_ref[...] = s0_ref[...]

    expand_b = (_iota((GLA_KW, GLA_W), 0) // GLA_DK == _iota((GLA_KW, GLA_W), 1) // GLA_DV).astype(BF16)
    own_block = (_iota((GLA_W, GLA_KW), 0) // GLA_DV == _iota((GLA_W, GLA_KW), 1) // GLA_DK)
    e64 = _block_ones(GLA_W, GLA_DV)
    n = nb * c
    nsub = c // sub
    tn, sn = _iota((n, n), 0), _iota((n, n), 1)
    tri = ((sn <= tn) & (sn // sub == tn // sub)).astype(BF16)
    row = _iota((sub, 1), 0)
    rown = _iota((n, 1), 0)
    pg = jnp.concatenate([pg_ref[bi] for bi in range(nb)], axis=0)
    q = pg[:, 0:128] * (GLA_DK ** -0.5)
    k = pg[:, 128:256]
    v = pg[:, 256:512]
    rg = pg[:, 512:768]
    z = _mm(pg[:, 768:768 + GLA_A_LORA], a2_ref[...]) + ab_ref[...]
    la = (jnp.minimum(z, 0.0) - jnp.log(1.0 + jnp.exp(-jnp.abs(z)))) / GLA_TAU
    if t_valid < c:
        valid = rown % c < t_valid
        la = jnp.where(valid, la, 0.0)
        k = jnp.where(valid, k, 0.0)
    bcum = _mm2l(tri, la)
    qdec = q * jnp.exp(bcum)

    blocks = [(bi, j) for bi in range(nb) for j in range(nsub)]
    cut = lambda t: [t[bi * c + j * sub:bi * c + (j + 1) * sub] for bi, j in blocks]
    q_b, k_b, v_b, b_b, qd_b = cut(q), cut(k), cut(v), cut(bcum), cut(qdec)
    amat = []
    for qq, kk, bb in zip(q_b, k_b, b_b):
        prods = [jnp.where(row >= s, qq * kk[s:s + 1] * jnp.exp(jnp.minimum(bb - bb[s:s + 1], 0.0)), 0.0)
                 for s in range(sub)]
        amat.append(_mm(jnp.concatenate(prods, axis=0), expand_b))
    upd = [jnp.where(own_block, _mm(vv, kk * jnp.exp(bb[sub - 1:sub] - bb), 0, 0), 0.0)
           for vv, kk, bb in zip(v_b, k_b, b_b)]
    states = []
    for bi in range(nb):
        st = s_ref[bi]
        for j in range(nsub):
            states.append(st)
            i = bi * nsub + j
            st = st * jnp.exp(b_b[i][sub - 1:sub]) + upd[i]
        s_ref[bi] = st
    outs = []
    for i in range(len(blocks)):
        o = _mm(qd_b[i], states[i], 1, 1)
        for s in range(sub):
            o = o + amat[i][s * sub:(s + 1) * sub] * v_b[i][s:s + 1]
        outs.append(o)
    o = jnp.concatenate(outs, axis=0)
    ms = _mm2(o * o, e64) * (1.0 / GLA_DV)
    y = o * lax.rsqrt(ms + EPS) * ng_ref[...] * (rg * _sigmoid(rg))
    for bi in range(nb):
        y_ref[bi] = y[bi * c:(bi + 1) * c]


def gla_mixer(pg, s0t, p, t_valid):
    b, t, _ = pg.shape
    sub = min(GLA_SUB, t)
    c = min(4 * sub, t)
    nb = min(GLA_BATCH, b)
    tok = lambda w: pl.BlockSpec((nb, c, w), lambda i, j: (i, j, 0))
    st_spec = pl.BlockSpec((nb, GLA_W, GLA_KW), lambda i, j: (i, 0, 0))
    ng = jnp.tile(p['gla_norm_g'], GLA_HEADS).reshape(1, GLA_W)
    return pl.pallas_call(
        functools.partial(_gla_kernel, nb, c, sub, t_valid),
        grid=(b // nb, t // c),
        in_specs=[tok(GLA_PAD), st_spec, _full((GLA_A_LORA, GLA_KW)), _full((1, GLA_KW)), _full((1, GLA_W))],
        out_specs=[tok(GLA_W), st_spec],
        out_shape=[jax.ShapeDtypeStruct((b, t, GLA_W), F32), jax.ShapeDtypeStruct((b, GLA_W, GLA_KW), F32)],
        compiler_params=_cparams(("parallel", "arbitrary")),
    )(pg, s0t, p['gla_a2'], p['gla_ab'].reshape(1, GLA_KW), ng)


def _s5_kernel(rows, seg, carry_state, u_ref, hre_ref, him_ref, pre_ref, pim_ref, bre_ref, bim_ref,
               cre_ref, cim_ref, d_ref, ga_ref, gb_ref, y_ref, ore_ref, oim_ref, *scratch):
    u = u_ref[...]
    drive = _mm if carry_state else _mm3
    h_re = drive(u, bre_ref[...])
    h_im = drive(u, bim_ref[...])
    tpos = _iota((rows, 1), 0) % seg
    d = 1
    while d < seg:
        ar, ai = pre_ref[d - 1:d], pim_ref[d - 1:d]
        sr = jnp.where(tpos >= d, pltpu.roll(h_re, shift=d, axis=0), 0.0)
        sm = jnp.where(tpos >= d, pltpu.roll(h_im, shift=d, axis=0), 0.0)
        h_re, h_im = h_re + ar * sr - ai * sm, h_im + ar * sm + ai * sr
        d *= 2
    if carry_state:
        cre_s, cim_s = scratch

        @pl.when(pl.program_id(1) == 0)
        def _():
            cre_s[...] = hre_ref[0]
            cim_s[...] = him_ref[0]
        i_re, i_im = cre_s[...], cim_s[...]
        pr, pi = pre_ref[0:seg], pim_ref[0:seg]
        parts_re, parts_im = [], []
        for k in range(rows // seg):
            sl = slice(k * seg, (k + 1) * seg)
            p_re = h_re[sl] + pr * i_re - pi * i_im
            p_im = h_im[sl] + pr * i_im + pi * i_re
            i_re, i_im = p_re[seg - 1:seg], p_im[seg - 1:seg]
            parts_re.append(p_re)
            parts_im.append(p_im)
        h_re, h_im = jnp.concatenate(parts_re, axis=0), jnp.concatenate(parts_im, axis=0)
        cre_s[...] = i_re
        cim_s[...] = i_im
        ore_ref[0] = i_re
        oim_ref[0] = i_im
    else:
        i_re, i_im = hre_ref[...], him_ref[...]
        pr, pi = pre_ref[...], pim_ref[...]
        h_re, h_im = h_re + pr * i_re - pi * i_im, h_im + pr * i_im + pi * i_re
        ore_ref[...] = h_re
        oim_ref[...] = h_im
    y = _mm(h_re, cre_ref[...]) - _mm(h_im, cim_ref[...]) + d_ref[...] * u
    z = 0.5 * y * (1.0 + jnp.tanh(0.7978845608028654 * (y + 0.044715 * y * y * y)))
    y_ref[...] = _mm(z, ga_ref[...]) * _sigmoid(_mm(z, gb_ref[...]))


def s5_mixer(u, h0_re, h0_im, p, seq_len):
    n = u.shape[0]
    b = n // seq_len
    f32 = F32
    a_re, a_im = p['s5_a_re'], p['s5_a_im']
    dt = jnp.exp(p['s5_log_dt'])[:, None]
    mag = jnp.exp(a_re * dt)
    ab_re, ab_im = mag * jnp.cos(a_im * dt), mag * jnp.sin(a_im * dt)
    den = a_re * a_re + a_im * a_im
    cr = ((ab_re - 1.0) * a_re + ab_im * a_im) / den
    ci = (ab_im * a_re - (ab_re - 1.0) * a_im) / den
    bb_re = cr[..., None] * p['s5_b_re'] - ci[..., None] * p['s5_b_im']
    bb_im = cr[..., None] * p['s5_b_im'] + ci[..., None] * p['s5_b_re']
    eye_g = jnp.eye(S5_GROUPS, dtype=f32)

    def in_blockdiag(w):
        return jnp.einsum('gpi,gh->gihp', w, eye_g).reshape(S5_W, S5_STATE)

    def out_blockdiag(w):
        return jnp.einsum('gip,gh->gphi', w, eye_g).reshape(S5_STATE, S5_W)

    carry_state = seq_len > S5_TILE
    rows = S5_TILE
    seg = SUBLANE if carry_state else seq_len
    pw_re, pw_im = ab_re.reshape(1, S5_STATE), ab_im.reshape(1, S5_STATE)
    while pw_re.shape[0] < seg:
        tr, ti_ = pw_re[-1:], pw_im[-1:]
        pw_re, pw_im = (jnp.concatenate([pw_re, pw_re * tr - pw_im * ti_], axis=0),
                        jnp.concatenate([pw_im, pw_re * ti_ + pw_im * tr], axis=0))
    pw_re, pw_im = jnp.tile(pw_re, (rows // seg, 1)), jnp.tile(pw_im, (rows // seg, 1))
    weights = [pw_re, pw_im, in_blockdiag(bb_re), in_blockdiag(bb_im),
               out_blockdiag(p['s5_c_re']), out_blockdiag(p['s5_c_im']),
               p['s5_d'].reshape(1, S5_W), p['s5_glu_a'], p['s5_glu_b']]
    wspecs = [_full(w.shape) for w in weights]
    if carry_state:
        tiles = seq_len // rows
        grid = (b, tiles)
        tok = lambda w: pl.BlockSpec((rows, w), lambda i, j: (i * tiles + j, 0))
        st_in = pl.BlockSpec((1, 1, S5_STATE), lambda i, j: (i, 0, 0))
        ins = [u, h0_re.reshape(b, 1, S5_STATE), h0_im.reshape(b, 1, S5_STATE)]
        st_shape = jax.ShapeDtypeStruct((b, 1, S5_STATE), f32)
        scratch = [pltpu.VMEM((1, S5_STATE), f32)] * 2
        sem = ("parallel", "arbitrary")
        st_out = st_in
    else:
        grid = (n // rows,)
        tok = lambda w: pl.BlockSpec((rows, w), lambda i: (i, 0))
        st_in = tok(S5_STATE)
        ins = [u, jnp.repeat(h0_re, seq_len, axis=0), jnp.repeat(h0_im, seq_len, axis=0)]
        st_shape = jax.ShapeDtypeStruct((n, S5_STATE), f32)
        scratch = []
        sem = ("parallel",)
        st_out = st_in
    y, o_re, o_im = pl.pallas_call(
        functools.partial(_s5_kernel, rows, seg, carry_state),
        grid=grid,
        in_specs=[tok(S5_W), st_in, st_in] + wspecs,
        out_specs=[tok(S5_W), st_out, st_out],
        out_shape=[jax.ShapeDtypeStruct((n, S5_W), f32), st_shape, st_shape],
        scratch_shapes=scratch,
        compiler_params=_cparams(sem),
    )(*ins, *weights)
    if carry_state:
        return y, o_re.reshape(b, S5_STATE), o_im.reshape(b, S5_STATE)
    last = lambda h: h.reshape(b, seq_len, S5_STATE)[:, -1]
    return y, last(o_re), last(o_im)


def _rope_rotate(x, cos_t, sin_a, sin_b):
    w = x.shape[1]
    return x * cos_t + pltpu.roll(x, shift=w - MLA_ROPE // 2, axis=1) * sin_a + pltpu.roll(x, shift=MLA_ROPE // 2, axis=1) * sin_b


def _mla_keys(c, kr, cos_t, sin_a, sin_b, wuk_ref, pk_ref):
    k_nope = _mm(c, wuk_ref[...])
    k_rope = _rope_rotate(_mm3(kr, pk_ref[...]), cos_t, sin_a, sin_b)
    return k_nope, k_rope


def _mla_prep_kernel(pm_ref, cos_ref, sina_ref, sinb_ref, qag_ref, wuq_ref, qg_ref, kvag_ref,
                     wuk_ref, wuv_ref, pk_ref, q_ref, k_ref, v_ref, c_ref, kr_ref):
    pm = pm_ref[...]
    qd, kvd, kr = pm[:, 0:256], pm[:, 256:512], pm[:, 512:512 + MLA_ROPE]
    tile4 = lambda t: jnp.concatenate([t] * MLA_HEADS, axis=1)
    cos_t, sin_a, sin_b = tile4(cos_ref[...]), tile4(sina_ref[...]), tile4(sinb_ref[...])
    e128 = _block_ones(MLA_HW, MLA_HP)
    qn = qd * lax.rsqrt(jnp.mean(qd * qd, axis=-1, keepdims=True) + EPS) * qag_ref[...]
    q = _mm(qn, wuq_ref[...])
    q = q * lax.rsqrt(_mm2(q * q, e128) * (1.0 / MLA_QK) + EPS) * qg_ref[...]
    q = _rope_rotate(q, cos_t, sin_a, sin_b)
    c = kvd * lax.rsqrt(jnp.mean(kvd * kvd, axis=-1, keepdims=True) + EPS) * kvag_ref[...]
    c_ref[...] = c
    kr_ref[...] = kr
    k_nope, k_rope = _mla_keys(c, kr, cos_t, sin_a, sin_b, wuk_ref, pk_ref)
    ssq = _mm2(k_nope * k_nope, e128) + jnp.sum(kr * kr, axis=-1, keepdims=True)
    kfull = (k_nope + k_rope) * (lax.rsqrt(ssq * (1.0 / MLA_QK) + EPS) * MLA_SCALE)
    vfull = _mm(c, wuv_ref[...])
    for h in range(MLA_HEADS):
        hs = slice(h * MLA_HP, (h + 1) * MLA_HP)
        q_ref[h] = _bf(q[:, hs])
        k_ref[h] = _bf(kfull[:, hs])
        v_ref[h] = _bf(vfull[:, hs])


def _flash_kernel(t, q_ref, k_ref, v_ref, o_ref):
    iq = pl.program_id(2)
    q = q_ref[0]

    def step(ik, carry, mask):
        m, l, acc = carry
        off = pl.multiple_of(ik * t, t)
        s = _dg(q, k_ref[0, pl.ds(off, t), :], 1, 1)
        if mask is not None:
            s = jnp.where(mask, s, NEG_INF)
        m_new = jnp.maximum(m, jnp.max(s, axis=-1, keepdims=True))
        alpha = jnp.exp(m - m_new)
        pexp = jnp.exp(s - m_new)
        l = alpha * l + jnp.sum(pexp, axis=-1, keepdims=True)
        acc = alpha * acc + jnp.dot(_bf(pexp), v_ref[0, pl.ds(off, t), :], preferred_element_type=F32)
        return m_new, l, acc

    init = (jnp.full((t, 1), NEG_INF, F32), jnp.zeros((t, 1), F32), jnp.zeros((t, MLA_HP), F32))
    carry = lax.fori_loop(0, iq, lambda ik, cr: step(ik, cr, None), init)
    m, l, acc = step(iq, carry, _iota((t, t), 1) <= _iota((t, t), 0))
    o_ref[...] = acc / l


def mla_prompt(pm, p, batch, seq_len, tables):
    n = pm.shape[0]
    rows = min(ROW_TILE, seq_len)
    tiles = seq_len // rows
    cos_t, sin_a, sin_b = tables
    tok = lambda w: pl.BlockSpec((rows, w), lambda i: (i, 0))
    tab = pl.BlockSpec((rows, MLA_HP), lambda i: (i % tiles, 0))
    hd = pl.BlockSpec((MLA_HEADS, rows, MLA_HP), lambda i: (0, i, 0))
    weights = [p['qag'], p['wuq_pad'], p['qg_pad'], p['kvag'], p['wuk_pad'], p['wuv_pad'], p['pk']]
    hshape = jax.ShapeDtypeStruct((MLA_HEADS, n, MLA_HP), BF16)
    q, k, v, c, kr = pl.pallas_call(
        _mla_prep_kernel,
        grid=(n // rows,),
        in_specs=[tok(MLA_PAD), tab, tab, tab] + [_full(w.shape) for w in weights],
        out_specs=[hd, hd, hd, tok(MLA_LORA), tok(MLA_ROPE)],
        out_shape=[hshape, hshape, hshape, jax.ShapeDtypeStruct((n, MLA_LORA), F32),
                   jax.ShapeDtypeStruct((n, MLA_ROPE), F32)],
        compiler_params=_cparams(("parallel",)),
    )(pm, cos_t, sin_a, sin_b, *weights)
    tq = min(ATT_T, seq_len)
    nq = seq_len // tq
    kv_spec = pl.BlockSpec((1, seq_len, MLA_HP), lambda b, h, i: (h, b, 0))
    o = pl.pallas_call(
        functools.partial(_flash_kernel, tq),
        grid=(batch, MLA_HEADS, nq),
        in_specs=[pl.BlockSpec((1, tq, MLA_HP), lambda b, h, i: (h, b * nq + i, 0)), kv_spec, kv_spec],
        out_specs=pl.BlockSpec((tq, MLA_HP), lambda b, h, i: (b * nq + i, h)),
        out_shape=jax.ShapeDtypeStruct((n, MLA_HW), F32),
        compiler_params=_cparams(("parallel", "parallel", "arbitrary")),
    )(q, k, v)
    return o, c, kr


def _mla_decode_kernel(n_pages, t_new, *refs):
    (pm_ref, cosn_ref, sinan_ref, sinbn_ref, cosp_ref, sinap_ref, sinbp_ref,
     qag_ref, wuq_ref, qg_ref, kvag_ref, wukp_ref, wuv_ref, pk_ref, wuk_ref, kgain_ref) = refs[1:17]
    c_refs = refs[17:17 + n_pages]
    kr_refs = refs[17 + n_pages:17 + 2 * n_pages]
    o_ref, c_out, kr_out, q_s, qn_s, qr_s, m_s, l_s, acc_s, cbf_s = refs[17 + 2 * n_pages:]
    j = pl.program_id(1)
    nq = MLA_HEADS * SUBLANE
    n = n_pages * PAGE_SIZE
    nr = n // 4
    pr = PAGE_SIZE // 4
    tile4 = lambda t: jnp.concatenate([t] * MLA_HEADS, axis=1)
    row_head = (_iota((nq, MLA_HW), 0) // SUBLANE == _iota((nq, MLA_HW), 1) // MLA_HP)
    head_sel = (_iota((nq, MLA_LORA), 0) // SUBLANE == _iota((nq, MLA_LORA), 1) // MLA_NOPE)

    def softmax_step(s, values):
        m_old = m_s[...]
        m_new = jnp.maximum(m_old, jnp.max(s, axis=-1, keepdims=True))
        alpha = jnp.exp(m_old - m_new)
        pexp = jnp.exp(s - m_new)
        l_s[...] = alpha * l_s[...] + jnp.sum(pexp, axis=-1, keepdims=True)
        m_s[...] = m_new
        acc_s[...] = acc_s[...] * alpha + _mm(pexp, values)

    pm = pm_ref[0]
    qd, kvd, kr_new = pm[:, 0:256], pm[:, 256:512], pm[:, 512:512 + MLA_ROPE]
    c_new = kvd * lax.rsqrt(jnp.mean(kvd * kvd, axis=-1, keepdims=True) + EPS) * kvag_ref[...]
    cos_n, sin_an, sin_bn = tile4(cosn_ref[...]), tile4(sinan_ref[...]), tile4(sinbn_ref[...])

    @pl.when(j == 0)
    def _():
        e128 = _block_ones(MLA_HW, MLA_HP)
        qn = qd * lax.rsqrt(jnp.mean(qd * qd, axis=-1, keepdims=True) + EPS) * qag_ref[...]
        q = _mm(qn, wuq_ref[...])
        q = q * lax.rsqrt(_mm2(q * q, e128) * (1.0 / MLA_QK) + EPS) * qg_ref[...]
        q = _rope_rotate(q, cos_n, sin_an, sin_bn)
        q4 = jnp.concatenate([q] * MLA_HEADS, axis=0)
        q_s[...] = jnp.where(row_head, q4, 0.0)
        li, lo = _iota((MLA_HW, MLA_LORA), 0), _iota((MLA_HW, MLA_LORA), 1)
        take_nope = ((li // MLA_HP == lo // MLA_NOPE) & (li % MLA_HP == lo % MLA_NOPE)).astype(BF16)
        qn_s[...] = _bf(jnp.where(head_sel, _mm(q4, take_nope), 0.0))
        ri, ro = _iota((MLA_HW, LANE), 0), _iota((MLA_HW, LANE), 1)
        take_rope = (ri % MLA_HP == MLA_NOPE + ro % MLA_ROPE).astype(BF16)
        qr = _mm(jnp.where(row_head, q4, 0.0), take_rope)
        qr4 = jnp.concatenate([qr] * 4, axis=0)
        own = _iota((LANE, LANE), 0) // nq == _iota((LANE, LANE), 1) // MLA_ROPE
        qr_s[...] = _bf(jnp.where(own, qr4, 0.0))
        m_s[...] = jnp.full(m_s.shape, NEG_INF, F32)
        l_s[...] = jnp.zeros(l_s.shape, F32)
        acc_s[...] = jnp.zeros(acc_s.shape, F32)
        c_out[0] = c_new
        kr_out[0] = kr_new

    for g in range(n_pages):
        for i in range(4):
            for half in range(2):
                rows = c_refs[g][0, 0, pl.ds(2 * i + half, pr, stride=8), :]
                cbf_s[i * nr + g * pr:i * nr + (g + 1) * pr, half * LANE:(half + 1) * LANE] = _bf(rows)
    kp = jnp.concatenate([kr_refs[g][0, 0] for g in range(n_pages)], axis=0)
    kg = kp * kgain_ref[...]
    krot = (kg * cosp_ref[...] + pltpu.roll(kg, shift=LANE - MLA_ROPE // 2, axis=1) * sinap_ref[...]
            + pltpu.roll(kg, shift=MLA_ROPE // 2, axis=1) * sinbp_ref[...])
    slot_sel = (_iota((LANE, LANE), 0) // nq == _iota((LANE, LANE), 1) // MLA_ROPE).astype(BF16)
    g_rope = _dg(qr_s[...], _bf(krot), 1, 1)
    g_sq = _dg(slot_sel, _bf(kp * kp), 1, 1)
    unpack = lambda t: jnp.concatenate([t[i * nq:(i + 1) * nq] for i in range(4)], axis=1)
    s_rope, sq_rope = unpack(g_rope), unpack(g_sq)
    nt = n // DEC_SUB
    sel_b = _bf(head_sel.astype(F32))
    cb = [cbf_s[i * DEC_SUB:(i + 1) * DEC_SUB, :] for i in range(nt)]
    kpre = [jnp.dot(x, wuk_ref[...], preferred_element_type=F32) for x in cb]
    ksq = [_bf(x * x) for x in kpre]
    kb = [_bf(x) for x in kpre]
    ssq = [_dg(sel_b, x, 1, 1) for x in ksq]
    sc = [_dg(qn_s[...], x, 1, 1) for x in kb]
    lanes = lambda t, i: t[:, i * DEC_SUB:(i + 1) * DEC_SUB]
    s = [(sc[i] + lanes(s_rope, i)) * (lax.rsqrt((ssq[i] + lanes(sq_rope, i)) * (1.0 / MLA_QK) + EPS) * MLA_SCALE)
         for i in range(nt)]
    m_old = m_s[...]
    m_new = m_old
    for x in s:
        m_new = jnp.maximum(m_new, jnp.max(x, axis=-1, keepdims=True))
    alpha = jnp.exp(m_old - m_new)
    pexp = [jnp.exp(x - m_new) for x in s]
    l_new = alpha * l_s[...]
    acc = acc_s[...] * alpha
    for x, values in zip(pexp, cb):
        l_new = l_new + jnp.sum(x, axis=-1, keepdims=True)
        acc = acc + jnp.dot(_bf(x), values, preferred_element_type=F32)
    m_s[...] = m_new
    l_s[...] = l_new
    acc_s[...] = acc

    @pl.when(j == pl.num_programs(1) - 1)
    def _():
        k_nope, k_rope = _mla_keys(c_new, kr_new, cos_n, sin_an, sin_bn, wukp_ref, pk_ref)
        ssq_n = _mm2l(row_head.astype(BF16), k_nope * k_nope, 1, 1)
        ssq_n = ssq_n + _mm2l(jnp.ones((SUBLANE, MLA_ROPE), BF16), kr_new * kr_new, 1, 1)[0:1]
        s_n = _mm(q_s[...], k_nope + k_rope, 1, 1) * (lax.rsqrt(ssq_n * (1.0 / MLA_QK) + EPS) * MLA_SCALE)
        qry_t = _iota((nq, SUBLANE), 0) % SUBLANE
        key_t = _iota((nq, SUBLANE), 1)
        softmax_step(jnp.where((key_t <= qry_t) & (key_t < t_new), s_n, NEG_INF), c_new)
        o_lat = acc_s[...] / l_s[...]
        full = jnp.where(row_head, _mm(o_lat, wuv_ref[...]), 0.0)
        gather = (_iota((SUBLANE, nq), 0) == _iota((SUBLANE, nq), 1) % SUBLANE).astype(BF16)
        o_ref[0] = _mm2l(gather, full)


def mla_decode(pm, cache_c, cache_kr4, page_table, layer, p, tables_new, tables_past4, t_new):
    b = pm.shape[0]
    pages = page_table.shape[1]
    steps = pages // DEC_PAGES
    n = DEC_PAGES * PAGE_SIZE
    per_b = lambda w: pl.BlockSpec((1, SUBLANE, w), lambda i, j, pt: (i, 0, 0))
    tab_new = pl.BlockSpec((SUBLANE, MLA_HP), lambda i, j, pt: (0, 0))
    tab_past = pl.BlockSpec((n // 4, LANE), lambda i, j, pt: (j, 0))
    weights = [p['qag'], p['wuq_pad'], p['qg_pad'], p['kvag'], p['wuk_pad'], p['wuv_pad'], p['pk'],
               p['wuk'], p['kgain4']]
    wspecs = [pl.BlockSpec(w.shape, lambda i, j, pt, nd=w.ndim: (0,) * nd) for w in weights]

    def page_spec(rows, w, slot):
        return pl.BlockSpec((1, 1, rows, w), lambda i, j, pt: (layer, pt[i, j * DEC_PAGES + slot], 0, 0))

    c_specs = [page_spec(PAGE_SIZE * MLA_LORA // LANE, LANE, s) for s in range(DEC_PAGES)]
    kr_specs = [page_spec(PAGE_SIZE // 4, LANE, s) for s in range(DEC_PAGES)]
    nq = MLA_HEADS * SUBLANE
    grid_spec = pltpu.PrefetchScalarGridSpec(
        num_scalar_prefetch=1,
        grid=(b, steps),
        in_specs=[per_b(MLA_PAD), tab_new, tab_new, tab_new, tab_past, tab_past, tab_past]
        + wspecs + c_specs + kr_specs,
        out_specs=[per_b(MLA_HW), per_b(MLA_LORA), per_b(MLA_ROPE)],
        scratch_shapes=[pltpu.VMEM((nq, MLA_HW), F32), pltpu.VMEM((nq, MLA_LORA), BF16),
                        pltpu.VMEM((LANE, LANE), BF16), pltpu.VMEM((nq, 1), F32),
                        pltpu.VMEM((nq, 1), F32), pltpu.VMEM((nq, MLA_LORA), F32),
                        pltpu.VMEM((n, MLA_LORA), BF16)],
    )
    o, c, kr = pl.pallas_call(
        functools.partial(_mla_decode_kernel, DEC_PAGES, t_new),
        grid_spec=grid_spec,
        out_shape=[jax.ShapeDtypeStruct((b, SUBLANE, MLA_HW), F32),
                   jax.ShapeDtypeStruct((b, SUBLANE, MLA_LORA), F32),
                   jax.ShapeDtypeStruct((b, SUBLANE, MLA_ROPE), F32)],
        compiler_params=_cparams(("parallel", "arbitrary")),
    )(page_table, pm, *tables_new, *tables_past4, *weights,
      *([cache_c] * DEC_PAGES), *([cache_kr4] * DEC_PAGES))
    return o, c, kr


def rope_tables_packed(pos):
    half = MLA_ROPE // 2
    inv = ROPE_THETA ** (-jnp.arange(half, dtype=F32) / half)
    ang = pos.astype(F32)[:, None] * inv
    cos, sin = jnp.cos(ang), jnp.sin(ang)
    zh = jnp.zeros_like(sin)
    pack = lambda t: t.reshape(pos.shape[0] // 4, LANE)
    return (pack(jnp.concatenate([cos, cos], axis=1)), pack(jnp.concatenate([-sin, zh], axis=1)),
            pack(jnp.concatenate([zh, sin], axis=1)))


def rope_tables(pos):
    half = MLA_ROPE // 2
    inv = ROPE_THETA ** (-jnp.arange(half, dtype=F32) / half)
    ang = pos.astype(F32)[:, None] * inv
    cos, sin = jnp.cos(ang), jnp.sin(ang)
    n = pos.shape[0]
    ones, zeros = jnp.ones((n, MLA_NOPE), F32), jnp.zeros((n, MLA_NOPE), F32)
    zpad = jnp.zeros((n, MLA_HP - MLA_QK), F32)
    zh = jnp.zeros((n, half), F32)
    cos_t = jnp.concatenate([ones, cos, cos, zpad + 1.0], axis=1)
    sin_a = jnp.concatenate([zeros, -sin, zh, zpad], axis=1)
    sin_b = jnp.concatenate([zeros, zh, sin, zpad], axis=1)
    return cos_t, sin_a, sin_b


def _merge_kernel(x_ref, xn_ref, yrw_ref, ygla_ref, ys5_ref, ymla_ref, wg_ref, wb_ref, wbm_ref, wo_ref, o_ref):
    xb = _bf(xn_ref[...])
    merged = None
    for n, y_ref in enumerate((yrw_ref, ygla_ref, ys5_ref, ymla_ref)):
        gate = _sigmoid(jnp.dot(xb, wg_ref[:, n * D_MODEL:(n + 1) * D_MODEL], preferred_element_type=F32))
        w = wbm_ref[...] if n == 3 else wb_ref[n]
        term = gate * jnp.dot(_bf(y_ref[...]), w, preferred_element_type=F32)
        merged = term if merged is None else merged + term
    o_ref[...] = x_ref[...] + jnp.dot(_bf(merged), wo_ref[...], preferred_element_type=F32)


def merge(x, xn, y_rw, y_gla, y_s5, y_mla, p):
    n = x.shape[0]
    row = lambda w: pl.BlockSpec((ROW_TILE, w), lambda i: (i, 0))
    weights = [p['w_gate'], p['w_branch3'], p['w_branch_mla'], p['w_out']]
    return pl.pallas_call(
        _merge_kernel,
        grid=(n // ROW_TILE,),
        in_specs=[row(D_MODEL), row(D_MODEL), row(BRANCH_W), row(BRANCH_W), row(BRANCH_W), row(MLA_HW)]
        + [_full(w.shape) for w in weights],
        out_specs=row(D_MODEL),
        out_shape=jax.ShapeDtypeStruct((n, D_MODEL), F32),
        compiler_params=_cparams(("parallel",)),
    )(x, xn, y_rw, y_gla, y_s5, y_mla, *weights)


def _mlp_kernel(ff_chunk, x_ref, g_ref, w1_ref, w2_ref, o_ref):
    x = x_ref[...]
    xb = _bf(x * lax.rsqrt(jnp.mean(x * x, axis=-1, keepdims=True) + EPS) * g_ref[...])
    acc = x
    for o in range(0, D_FF, ff_chunk):
        h = jnp.maximum(jnp.dot(xb, w1_ref[:, o:o + ff_chunk], preferred_element_type=F32), 0.0)
        acc = acc + jnp.dot(_bf(h * h), w2_ref[o:o + ff_chunk, :], preferred_element_type=F32)
    o_ref[...] = acc


def mlp(x, p):
    n = x.shape[0]
    row = pl.BlockSpec((ROW_TILE, D_MODEL), lambda i: (i, 0))
    return pl.pallas_call(
        functools.partial(_mlp_kernel, 1024),
        grid=(n // ROW_TILE,),
        in_specs=[row, _full((1, D_MODEL)), _full((D_MODEL, D_FF)), _full((D_FF, D_MODEL))],
        out_specs=row,
        out_shape=jax.ShapeDtypeStruct((n, D_MODEL), F32),
        compiler_params=_cparams(("parallel",)),
    )(x, p['norm2_g'].reshape(1, D_MODEL), p['w_ff1'], p['w_ff2'])


def _pad_cols(w, width):
    return jnp.pad(w, ((0, 0), (0, width - w.shape[1])))


def _head_pad(w, used):
    kdim = w.shape[0]
    w = w.reshape(kdim, MLA_HEADS, used)
    return jnp.pad(w, ((0, 0), (0, 0), (0, MLA_HP - used))).reshape(kdim, MLA_HW)


def prepare_layer(l, W):
    p = {}
    w_in = W['w_in'][l]
    o_gla, o_s5, o_mla = RW_COLS, RW_COLS + GLA_COLS, RW_COLS + GLA_COLS + S5_W
    w_gla = w_in[:, o_gla:o_s5]
    w_gla = jnp.concatenate([w_gla[:, 0:512], w_gla[:, 528:784], w_gla[:, 512:528]], axis=1)
    p['w_rw'] = _bf(w_in[:, :RW_COLS])
    p['w_all'] = _bf(jnp.concatenate([w_in[:, :RW_COLS], _pad_cols(w_gla, GLA_PAD), w_in[:, o_s5:o_mla],
                                      _pad_cols(w_in[:, o_mla:], MLA_PAD)], axis=1))
    for name in ('norm1_g', 'rw_mu', 'rw_w0', 'rw_w2', 'rw_a0', 'rw_a2', 'rw_g2', 'rw_k_k', 'rw_k_a',
                 'rw_ln_g', 'rw_ln_b', 'gla_a2', 'gla_ab', 'gla_norm_g', 's5_a_re', 's5_a_im', 's5_log_dt',
                 's5_b_re', 's5_b_im', 's5_c_re', 's5_c_im', 's5_glu_a', 's5_glu_b', 'norm2_g'):
        p[name] = W[name][l]
    p['rw_r_k'] = W['rw_r_k'][l].reshape(RW_W)
    p['s5_d'] = W['s5_d'][l].reshape(S5_W)
    if l > 0:
        for name in ('rw_v0', 'rw_v1', 'rw_v2'):
            p[name] = W[name][l - 1]
    k_g = W['mla_k_g'][l]
    p['qag'] = W['mla_qa_g'][l].reshape(1, MLA_LORA)
    p['kvag'] = W['mla_kva_g'][l].reshape(1, MLA_LORA)
    p['wuq_pad'] = _head_pad(W['mla_w_uq'][l], MLA_QK)
    qg = W['mla_q_g'][l] * jnp.concatenate([k_g[:MLA_NOPE], jnp.ones((MLA_ROPE,), F32)])
    p['qg_pad'] = jnp.tile(jnp.pad(qg, (0, MLA_HP - MLA_QK)), MLA_HEADS).reshape(1, MLA_HW)
    p['wuk_pad'] = _head_pad(W['mla_w_uk'][l].reshape(MLA_LORA, MLA_HEADS * MLA_NOPE), MLA_NOPE)
    wuv = W['mla_w_uv'][l].reshape(MLA_LORA, MLA_HEADS * MLA_V)
    p['wuv_pad'] = _head_pad(wuv, MLA_V)
    p['wuk'] = _bf(W['mla_w_uk'][l].reshape(MLA_LORA, MLA_HEADS * MLA_NOPE))
    p['kgain4'] = jnp.tile(k_g[MLA_NOPE:], 4).reshape(1, LANE)
    place = jnp.pad(jnp.diag(k_g[MLA_NOPE:]), ((0, 0), (MLA_NOPE, MLA_HP - MLA_QK)))
    p['pk'] = jnp.tile(place, (1, MLA_HEADS))
    p['w_gate'] = _bf(W['w_gate'][l])
    wb = W['w_branch'][l]
    p['w_branch3'] = _bf(wb)
    wbm = wb[3].reshape(MLA_HEADS, MLA_V, D_MODEL)
    p['w_branch_mla'] = _bf(jnp.pad(wbm, ((0, 0), (0, MLA_HP - MLA_V), (0, 0))).reshape(MLA_HW, D_MODEL))
    p['w_out'] = _bf(W['w_out'][l])
    p['w_ff1'] = _bf(W['w_ff1'][l])
    p['w_ff2'] = _bf(W['w_ff2'][l])
    return p


def _gla_state_in(s):
    eye = jnp.eye(GLA_HEADS, dtype=F32)
    return jnp.einsum('bhdv,hg->bhvgd', s, eye).reshape(s.shape[0], GLA_W, GLA_KW)


def _gla_state_out(st):
    b = st.shape[0]
    st = st.reshape(b, GLA_HEADS, GLA_DV, GLA_HEADS, GLA_DK)
    return jnp.stack([st[:, h, :, h, :] for h in range(GLA_HEADS)], axis=1).swapaxes(2, 3)


def kernel(x_prompt, x_sample, cache_mla_latent, cache_mla_krope, page_table, state_rwkv, state_shift, state_gla, state_s5_re, state_s5_im, norm1_g, w_in, rw_mu, rw_w0, rw_w2, rw_a0, rw_a2, rw_g2, rw_v0, rw_v1, rw_v2, rw_k_k, rw_k_a, rw_r_k, rw_ln_g, rw_ln_b, gla_a2, gla_ab, gla_norm_g, s5_a_re, s5_a_im, s5_log_dt, s5_b_re, s5_b_im, s5_c_re, s5_c_im, s5_d, s5_glu_a, s5_glu_b, mla_qa_g, mla_w_uq, mla_kva_g, mla_w_uk, mla_w_uv, mla_q_g, mla_k_g, w_gate, w_branch, w_out, norm2_g, w_ff1, w_ff2):
    W = dict(norm1_g=norm1_g, w_in=w_in, rw_mu=rw_mu, rw_w0=rw_w0, rw_w2=rw_w2, rw_a0=rw_a0, rw_a2=rw_a2,
             rw_g2=rw_g2, rw_v0=rw_v0, rw_v1=rw_v1, rw_v2=rw_v2, rw_k_k=rw_k_k, rw_k_a=rw_k_a, rw_r_k=rw_r_k,
             rw_ln_g=rw_ln_g, rw_ln_b=rw_ln_b, gla_a2=gla_a2, gla_ab=gla_ab, gla_norm_g=gla_norm_g,
             s5_a_re=s5_a_re, s5_a_im=s5_a_im, s5_log_dt=s5_log_dt, s5_b_re=s5_b_re, s5_b_im=s5_b_im,
             s5_c_re=s5_c_re, s5_c_im=s5_c_im, s5_d=s5_d, s5_glu_a=s5_glu_a, s5_glu_b=s5_glu_b,
             mla_qa_g=mla_qa_g, mla_w_uq=mla_w_uq, mla_kva_g=mla_kva_g, mla_w_uk=mla_w_uk, mla_w_uv=mla_w_uv,
             mla_q_g=mla_q_g, mla_k_g=mla_k_g, w_gate=w_gate, w_branch=w_branch, w_out=w_out,
             norm2_g=norm2_g, w_ff1=w_ff1, w_ff2=w_ff2)
    bp, tp, _ = x_prompt.shape
    bs, ts, _ = x_sample.shape
    depth = w_in.shape[0]
    np_, ns = bp * tp, bs * ts
    past_len = page_table.shape[1] * PAGE_SIZE
    tpad = SUBLANE

    tables_p = rope_tables(jnp.arange(tp, dtype=jnp.int32))
    tables_new = rope_tables(past_len + jnp.arange(tpad, dtype=jnp.int32))
    tables_past = rope_tables_packed(jnp.arange(past_len, dtype=jnp.int32))
    cache_kr4 = cache_mla_krope.reshape(cache_mla_krope.shape[0], -1, PAGE_SIZE // 4, LANE)
    cache_c2 = cache_mla_latent.reshape(cache_mla_latent.shape[0], -1, PAGE_SIZE * MLA_LORA // LANE, LANE)

    xp, xs = x_prompt.reshape(np_, D_MODEL), x_sample.reshape(ns, D_MODEL)
    pad_t = lambda a: jnp.pad(a.reshape(bs, ts, a.shape[-1]), ((0, 0), (0, tpad - ts), (0, 0)))
    unpad = lambda a: a[:, :ts].reshape(ns, a.shape[-1])
    vf_p = vf_s = None
    outs = {k: [] for k in ('c_p', 'kr_p', 'c_s', 'kr_s', 'rw_p', 'rw_s', 'sh_p', 'sh_s', 'gla_p', 'gla_s',
                            're_p', 'im_p', 're_s', 'im_s')}
    zeros_p = lambda *shape: jnp.zeros((bp,) + shape, F32)
    for l in range(depth):
        p = prepare_layer(l, W)
        xn_p, prw_p, pgla_p, ps5_p, pmla_p = norm_proj(xp, p['norm1_g'], p['w_all'])
        xn_s, prw_s, pgla_s, ps5_s, pmla_s = norm_proj(xs, p['norm1_g'], p['w_all'])
        outs['sh_p'].append(xn_p.reshape(bp, tp, D_MODEL)[:, -1])
        outs['sh_s'].append(xn_s.reshape(bs, ts, D_MODEL)[:, -1])

        y_rw_p, vf_p, s_rw_p = rwkv_mixer(prw_p.reshape(bp, tp, RW_COLS), zeros_p(1, RW_COLS),
                                          zeros_p(RW_HEADS, RW_HD, RW_HD), vf_p, p, tp)
        shift_proj = small_matmul(state_shift[l], p['w_rw']).reshape(bs, 1, RW_COLS)
        y_rw_s, vf_s, s_rw_s = rwkv_mixer(pad_t(prw_s), shift_proj, state_rwkv[l], vf_s, p, ts)
        outs['rw_p'].append(s_rw_p)
        outs['rw_s'].append(s_rw_s)

        y_gla_p, s_gla_p = gla_mixer(pgla_p.reshape(bp, tp, GLA_PAD), zeros_p(GLA_W, GLA_KW), p, tp)
        y_gla_s, s_gla_s = gla_mixer(pad_t(pgla_s), _gla_state_in(state_gla[l]), p, ts)
        outs['gla_p'].append(_gla_state_out(s_gla_p))
        outs['gla_s'].append(_gla_state_out(s_gla_s))

        y_s5_p, re_p, im_p = s5_mixer(ps5_p, zeros_p(S5_STATE), zeros_p(S5_STATE), p, tp)
        y_s5_s, re_s, im_s = s5_mixer(ps5_s, state_s5_re[l].reshape(bs, S5_STATE),
                                      state_s5_im[l].reshape(bs, S5_STATE), p, ts)
        for key, val, bb in (('re_p', re_p, bp), ('im_p', im_p, bp), ('re_s', re_s, bs), ('im_s', im_s, bs)):
            outs[key].append(val.reshape(bb, S5_GROUPS, S5_P))

        y_mla_p, c_p, kr_p = mla_prompt(pmla_p, p, bp, tp, tables_p)
        y_mla_s, c_s, kr_s = mla_decode(pad_t(pmla_s), cache_c2, cache_kr4, page_table, l, p,
                                        tables_new, tables_past, ts)
        outs['c_p'].append(c_p.reshape(bp, tp, MLA_LORA))
        outs['kr_p'].append(kr_p.reshape(bp, tp, MLA_ROPE))
        outs['c_s'].append(c_s[:, :ts])
        outs['kr_s'].append(kr_s[:, :ts])

        xp = mlp(merge(xp, xn_p, y_rw_p.reshape(np_, RW_W), y_gla_p.reshape(np_, GLA_W), y_s5_p, y_mla_p, p), p)
        xs = mlp(merge(xs, xn_s, unpad(y_rw_s), unpad(y_gla_s), y_s5_s, unpad(y_mla_s), p), p)

    st = lambda k: jnp.stack(outs[k], axis=0)
    return (xp.reshape(bp, tp, D_MODEL), xs.reshape(bs, ts, D_MODEL),
            st('c_p'), st('kr_p'), st('c_s'), st('kr_s'),
            st('rw_p'), st('rw_s'), st('sh_p'), st('sh_s'),
            st('gla_p'), st('gla_s'), st('re_p'), st('im_p'), st('re_s'), st('im_s'))
```

```python
import functools

import jax
import jax.numpy as jnp
from jax import lax
from jax.experimental import pallas as pl
from jax.experimental.pallas import tpu as pltpu

F32 = jnp.float32
BF16 = jnp.bfloat16

D_MODEL = 1024
N_BRANCH = 4
BRANCH_W = 256
EPS = 1e-6
PAGE_SIZE = 128
RW_HEADS, RW_HD, RW_W = 4, 64, 256
RW_COLS = 896
RW_GN_EPS = 64e-5
GLA_HEADS, GLA_DK, GLA_DV, GLA_W = 4, 32, 64, 256
GLA_KW = GLA_HEADS * GLA_DK
GLA_A_LORA = 16
GLA_TAU = 16.0
GLA_COLS = 784
GLA_PAD = 896
S5_GROUP, S5_GROUPS, S5_P, S5_W = 16, 16, 64, 256
S5_STATE = S5_GROUPS * S5_P
MLA_HEADS, MLA_NOPE, MLA_ROPE, MLA_QK, MLA_V = 4, 64, 32, 96, 64
MLA_LORA = 256
MLA_COLS = 544
MLA_PAD = 640
MLA_HP = 128
MLA_HW = MLA_HEADS * MLA_HP
MLA_SCALE = MLA_QK ** -0.5
ROPE_THETA = 10000.0
NEG_INF = -1e30
D_FF = 4096
IN_PAD = RW_COLS + GLA_PAD + S5_W + MLA_PAD

V7X_VMEM_BYTES = 64 * 1024 * 1024
VMEM_LIMIT = V7X_VMEM_BYTES - 8 * 1024 * 1024
LANE = 128
SUBLANE = 8

ROW_TILE = 512
RW_CHUNK = 64
RW_BATCH = 4
GLA_SUB = 16
GLA_BATCH = 4
S5_TILE = 128
ATT_T = 512
DEC_PAGES = 16
DEC_SUB = 512


def _cparams(sem):
    return pltpu.CompilerParams(dimension_semantics=sem, vmem_limit_bytes=VMEM_LIMIT)


def _bf(x):
    return x.astype(BF16)


def _dg(a, b, ca, cb):
    return lax.dot_general(a, b, (((ca,), (cb,)), ((), ())), preferred_element_type=F32)


def _mm(a, b, ca=1, cb=0):
    return _dg(_bf(a), _bf(b), ca, cb)


def _split(x):
    hi = _bf(x)
    return hi, _bf(x - hi.astype(F32))


def _mm3(a, b, ca=1, cb=0):
    ah, al = _split(a)
    bh, bl = _split(b)
    return _dg(ah, bh, ca, cb) + _dg(al, bh, ca, cb) + _dg(ah, bl, ca, cb)


def _mm2(a, b01, ca=1, cb=0):
    ah, al = _split(a)
    return _dg(ah, b01, ca, cb) + _dg(al, b01, ca, cb)


def _mm2l(a01, b, ca=1, cb=0):
    bh, bl = _split(b)
    return _dg(a01, bh, ca, cb) + _dg(a01, bl, ca, cb)


def _sigmoid(x):
    return 1.0 / (1.0 + jnp.exp(-x))


def _softplus(x):
    return jnp.maximum(x, 0.0) + jnp.log(1.0 + jnp.exp(-jnp.abs(x)))


def _iota(shape, dim):
    return lax.broadcasted_iota(jnp.int32, shape, dim)


def _block_ones(n, blk, dtype=BF16):
    return (_iota((n, n), 0) // blk == _iota((n, n), 1) // blk).astype(dtype)


def _full(shape):
    nd = len(shape)
    return pl.BlockSpec(shape, lambda *_: (0,) * nd)


def _norm_proj_kernel(x_ref, g_ref, w_ref, xn_ref, rw_ref, gla_ref, s5_ref, mla_ref):
    x = x_ref[...]
    xn = x * lax.rsqrt(jnp.mean(x * x, axis=-1, keepdims=True) + EPS) * g_ref[...]
    xn_ref[...] = xn
    xb = _bf(xn)
    o = 0
    for ref in (rw_ref, gla_ref, s5_ref, mla_ref):
        w = ref.shape[1]
        ref[...] = jnp.dot(xb, w_ref[:, o:o + w], preferred_element_type=F32)
        o += w


def norm_proj(x, g, w_all):
    n = x.shape[0]
    widths = (RW_COLS, GLA_PAD, S5_W, MLA_PAD)
    row = lambda w: pl.BlockSpec((ROW_TILE, w), lambda i: (i, 0))
    return pl.pallas_call(
        _norm_proj_kernel,
        grid=(n // ROW_TILE,),
        in_specs=[row(D_MODEL), _full((1, D_MODEL)), _full((D_MODEL, IN_PAD))],
        out_specs=[row(D_MODEL)] + [row(w) for w in widths],
        out_shape=[jax.ShapeDtypeStruct((n, D_MODEL), F32)]
        + [jax.ShapeDtypeStruct((n, w), F32) for w in widths],
        compiler_params=_cparams(("parallel",)),
    )(x, g.reshape(1, D_MODEL), w_all)


def _matmul_kernel(a_ref, b_ref, o_ref):
    o_ref[...] = _mm(a_ref[...], b_ref[...])


def small_matmul(a, b):
    m, n = a.shape[0], b.shape[1]
    return pl.pallas_call(
        _matmul_kernel,
        out_shape=jax.ShapeDtypeStruct((m, n), F32),
        compiler_params=_cparams(None),
    )(a, b)


def _rwkv_kernel(has_vres, nb, c, t_valid, *refs):
    if has_vres:
        (prw_ref, sp_ref, s0_ref, vf_ref, mu_ref, w0_ref, w2_ref, a0_ref, a2_ref, g2_ref,
         v0_ref, v1_ref, v2_ref, kk_ref, ka_ref, rk_ref, lng_ref, lnb_ref,
         y_ref, s_ref, carry_ref) = refs
    else:
        (prw_ref, sp_ref, s0_ref, mu_ref, w0_ref, w2_ref, a0_ref, a2_ref, g2_ref,
         kk_ref, ka_ref, rk_ref, lng_ref, lnb_ref,
         y_ref, vout_ref, s_ref, carry_ref) = refs

    @pl.when(pl.program_id(1) == 0)
    def _():
        s_ref[...] = s0_ref[...]
        carry_ref[...] = sp_ref[...]

    n = nb * c
    rowc = _iota((c, 1), 0)
    rown = _iota((n, 1), 0)
    ti, si = _iota((c, c), 0), _iota((c, c), 1)
    tn, sn = _iota((n, n), 0), _iota((n, n), 1)
    tri = ((sn <= tn) & (sn // c == tn // c)).astype(BF16)
    e64 = _block_ones(RW_W, RW_HD)
    rows = lambda fn: jnp.concatenate([fn(bi) for bi in range(nb)], axis=0)

    p = rows(lambda bi: prw_ref[bi])
    prev = rows(lambda bi: jnp.where(rowc == 0, carry_ref[bi], pltpu.roll(prw_ref[bi], shift=1, axis=0)))
    for bi in range(nb):
        carry_ref[bi] = prw_ref[bi, c - 1:c, :]
    rw = p + (prev - p) * mu_ref[...]
    r, k, v = rw[:, 0:256], rw[:, 256:512], rw[:, 512:768]
    tail = rw[:, 768:896]
    wd, ad, gd = tail[:, 0:32], tail[:, 32:64], tail[:, 64:128]
    w_log = -_softplus(-(w0_ref[...] + _mm(jnp.tanh(wd), w2_ref[...]))) - 0.5
    logw = -jnp.exp(w_log)
    a = _sigmoid(a0_ref[...] + _mm(ad, a2_ref[...]))
    g = _mm(_sigmoid(gd), g2_ref[...])
    if has_vres:
        vf = rows(lambda bi: vf_ref[bi])
        v = v + (vf - v) * _sigmoid(v0_ref[...] + _mm(_mm(v, v1_ref[...]), v2_ref[...]))
    else:
        for bi in range(nb):
            vout_ref[bi] = v[bi * c:(bi + 1) * c]
    kk = k * kk_ref[...]
    kk = kk * lax.rsqrt(_mm2(kk * kk, e64) + 1e-12)
    k2 = k * (1.0 + (a - 1.0) * ka_ref[...])
    bv = kk * a
    if t_valid < c:
        valid = rown % c < t_valid
        logw = jnp.where(valid, logw, 0.0)
        kk = jnp.where(valid, kk, 0.0)
        bv = jnp.where(valid, bv, 0.0)
        k2 = jnp.where(valid, k2, 0.0)
        v = jnp.where(valid, v, 0.0)
    g_incl = _mm2l(tri, logw)
    g_last = rows(lambda bi: jnp.broadcast_to(g_incl[(bi + 1) * c - 1:(bi + 1) * c], (c, RW_W)))
    e_neg = jnp.exp(-g_incl)
    e_last = jnp.exp(g_last - g_incl)
    kq = kk * jnp.exp(g_incl - logw)
    rq = r * jnp.exp(g_incl)
    bd, kd = bv * e_neg, k2 * e_neg
    bl, kl = bv * e_last, k2 * e_last
    s_decay = jnp.exp(g_last)

    chains = [(bi, h) for bi in range(nb) for h in range(RW_HEADS)]
    cut = lambda t: [t[bi * c:(bi + 1) * c, h * RW_HD:(h + 1) * RW_HD] for bi, h in chains]
    kq_c, rq_c, bd_c, kd_c, bl_c, kl_c, v_c = cut(kq), cut(rq), cut(bd), cut(kd), cut(bl), cut(kl), cut(v)
    s0_c = [s_ref[bi, h] for bi, h in chains]
    stack = lambda xs, ys: [jnp.concatenate([x, y], axis=0) for x, y in zip(xs, ys)]
    bk_c = stack(bd_c, kd_c)
    kbk = [_mm3(x, y, 1, 1) for x, y in zip(kq_c, bk_c)]
    rbk = [_mm(x, y, 1, 1) for x, y in zip(rq_c, bk_c)]
    a_b = [jnp.where(si < ti, x[:, 0:c], 0.0) for x in kbk]
    a_k = [jnp.where(si < ti, x[:, c:2 * c], 0.0) for x in kbk]
    m_b = [jnp.where(si <= ti, x[:, 0:c], 0.0) for x in rbk]
    m_k = [jnp.where(si <= ti, x[:, c:2 * c], 0.0) for x in rbk]
    krs0 = [_mm(x, s, 1, 1) for x, s in zip(stack(kq_c, rq_c), s0_c)]
    amv = [_mm(x, y) for x, y in zip(stack(a_k, m_k), v_c)]
    eye = (si == ti).astype(F32)
    t_inv = [eye - x for x in a_b]
    pw = a_b
    lvl = 2
    while lvl < c:
        pw = [_mm(x, x) for x in pw]
        t_inv = [x + _mm(x, y) for x, y in zip(t_inv, pw)]
        lvl *= 2
    u = [-_mm(x, y[0:c] + z[0:c]) for x, y, z in zip(t_inv, krs0, amv)]
    yh = [y[c:2 * c] + _mm(mb, uu) + z[c:2 * c] for y, mb, uu, z in zip(krs0, m_b, u, amv)]
    upd = [_mm(x, y, 0, 0) for x, y in zip(stack(u, v_c), stack(bl_c, kl_c))]
    for (bi, h), s0, dd in zip(chains, s0_c, upd):
        s_ref[bi, h] = s0 * s_decay[(bi + 1) * c - 1:(bi + 1) * c, h * RW_HD:(h + 1) * RW_HD] + dd

    y = rows(lambda bi: jnp.concatenate(yh[bi * RW_HEADS:(bi + 1) * RW_HEADS], axis=1))
    mean = _mm2(y, e64) * (1.0 / RW_HD)
    yc = y - mean
    var = _mm2(yc * yc, e64) * (1.0 / RW_HD)
    bonus = _mm2(r * k2 * rk_ref[...], e64) * v
    out = (yc * lax.rsqrt(var + RW_GN_EPS) * lng_ref[...] + lnb_ref[...] + bonus) * g
    for bi in range(nb):
        y_ref[bi] = out[bi * c:(bi + 1) * c]


def rwkv_mixer(prw, shift_proj, s0, v_first, p, t_valid):
    b, t, _ = prw.shape
    c = min(RW_CHUNK, t)
    nb = min(RW_BATCH * max(1, RW_CHUNK // (4 * c)), b)
    has_vres = v_first is not None
    tok = lambda w: pl.BlockSpec((nb, c, w), lambda i, j: (i, j, 0))
    per_b = lambda shape: pl.BlockSpec((nb,) + shape, lambda i, j: (i,) + (0,) * len(shape))
    vec = lambda a: a.reshape(1, -1)
    ins = [prw, shift_proj, s0]
    specs = [tok(RW_COLS), per_b((1, RW_COLS)), per_b((RW_HEADS, RW_HD, RW_HD))]
    if has_vres:
        ins.append(v_first)
        specs.append(tok(RW_W))
    params = [vec(p['rw_mu']), vec(p['rw_w0']), p['rw_w2'], vec(p['rw_a0']), p['rw_a2'], p['rw_g2']]
    if has_vres:
        params += [vec(p['rw_v0']), p['rw_v1'], p['rw_v2']]
    params += [vec(p['rw_k_k']), vec(p['rw_k_a']), vec(p['rw_r_k']), vec(p['rw_ln_g']), vec(p['rw_ln_b'])]
    ins += params
    specs += [_full(a.shape) for a in params]
    y_shape = jax.ShapeDtypeStruct((b, t, RW_W), F32)
    s_shape = jax.ShapeDtypeStruct((b, RW_HEADS, RW_HD, RW_HD), F32)
    s_spec = per_b((RW_HEADS, RW_HD, RW_HD))
    if has_vres:
        out_shape, out_specs = [y_shape, s_shape], [tok(RW_W), s_spec]
    else:
        out_shape, out_specs = [y_shape, y_shape, s_shape], [tok(RW_W), tok(RW_W), s_spec]
    outs = pl.pallas_call(
        functools.partial(_rwkv_kernel, has_vres, nb, c, t_valid),
        grid=(b // nb, t // c),
        in_specs=specs,
        out_specs=out_specs,
        out_shape=out_shape,
        scratch_shapes=[pltpu.VMEM((nb, 1, RW_COLS), F32)],
        compiler_params=_cparams(("parallel", "arbitrary")),
    )(*ins)
    if has_vres:
        return outs[0], v_first, outs[1]
    return outs[0], outs[1], outs[2]


def _gla_kernel(nb, c, sub, t_valid, pg_ref, s0_ref, a2_ref, ab_ref, ng_ref, y_ref, s_ref):
    @pl.when(pl.program_id(1) == 0)
    def _():
        s_ref[...] = s0_ref[...]

    expand_b = (_iota((GLA_KW, GLA_W), 0) // GLA_DK == _iota((GLA_KW, GLA_W), 1) // GLA_DV).astype(BF16)
    own_block = (_iota((GLA_W, GLA_KW), 0) // GLA_DV == _iota((GLA_W, GLA_KW), 1) // GLA_DK)
    e64 = _block_ones(GLA_W, GLA_DV)
    n = nb * c
    nsub = c // sub
    tn, sn = _iota((n, n), 0), _iota((n, n), 1)
    tri = ((sn <= tn) & (sn // sub == tn // sub)).astype(BF16)
    row = _iota((sub, 1), 0)
    rown = _iota((n, 1), 0)
    pg = jnp.concatenate([pg_ref[bi] for bi in range(nb)], axis=0)
    q = pg[:, 0:128] * (GLA_DK ** -0.5)
    k = pg[:, 128:256]
    v = pg[:, 256:512]
    rg = pg[:, 512:768]
    z = _mm(pg[:, 768:768 + GLA_A_LORA], a2_ref[...]) + ab_ref[...]
    la = (jnp.minimum(z, 0.0) - jnp.log(1.0 + jnp.exp(-jnp.abs(z)))) / GLA_TAU
    if t_valid < c:
        valid = rown % c < t_valid
        la = jnp.where(valid, la, 0.0)
        k = jnp.where(valid, k, 0.0)
    bcum = _mm2l(tri, la)
    qdec = q * jnp.exp(bcum)

    blocks = [(bi, j) for bi in range(nb) for j in range(nsub)]
    cut = lambda t: [t[bi * c + j * sub:bi * c + (j + 1) * sub] for bi, j in blocks]
    q_b, k_b, v_b, b_b, qd_b = cut(q), cut(k), cut(v), cut(bcum), cut(qdec)
    amat = []
    for qq, kk, bb in zip(q_b, k_b, b_b):
        prods = [jnp.where(row >= s, qq * kk[s:s + 1] * jnp.exp(jnp.minimum(bb - bb[s:s + 1], 0.0)), 0.0)
                 for s in range(sub)]
        amat.append(_mm(jnp.concatenate(prods, axis=0), expand_b))
    upd = [jnp.where(own_block, _mm(vv, kk * jnp.exp(bb[sub - 1:sub] - bb), 0, 0), 0.0)
           for vv, kk, bb in zip(v_b, k_b, b_b)]
    states = []
    for bi in range(nb):
        st = s_ref[bi]
        for j in range(nsub):
            states.append(st)
            i = bi * nsub + j
            st = st * jnp.exp(b_b[i][sub - 1:sub]) + upd[i]
        s_ref[bi] = st
    outs = []
    for i in range(len(blocks)):
        o = _mm(qd_b[i], states[i], 1, 1)
        for s in range(sub):
            o = o + amat[i][s * sub:(s + 1) * sub] * v_b[i][s:s + 1]
        outs.append(o)
    o = jnp.concatenate(outs, axis=0)
    ms = _mm2(o * o, e64) * (1.0 / GLA_DV)
    y = o * lax.rsqrt(ms + EPS) * ng_ref[...] * (rg * _sigmoid(rg))
    for bi in range(nb):
        y_ref[bi] = y[bi * c:(bi + 1) * c]


def gla_mixer(pg, s0t, p, t_valid):
    b, t, _ = pg.shape
    sub = min(GLA_SUB, t)
    c = min(4 * sub, t)
    nb = min(GLA_BATCH, b)
    tok = lambda w: pl.BlockSpec((nb, c, w), lambda i, j: (i, j, 0))
    st_spec = pl.BlockSpec((nb, GLA_W, GLA_KW), lambda i, j: (i, 0, 0))
    ng = jnp.tile(p['gla_norm_g'], GLA_HEADS).reshape(1, GLA_W)
    return pl.pallas_call(
        functools.partial(_gla_kernel, nb, c, sub, t_valid),
        grid=(b // nb, t // c),
        in_specs=[tok(GLA_PAD), st_spec, _full((GLA_A_LORA, GLA_KW)), _full((1, GLA_KW)), _full((1, GLA_W))],
        out_specs=[tok(GLA_W), st_spec],
        out_shape=[jax.ShapeDtypeStruct((b, t, GLA_W), F32), jax.ShapeDtypeStruct((b, GLA_W, GLA_KW), F32)],
        compiler_params=_cparams(("parallel", "arbitrary")),
    )(pg, s0t, p['gla_a2'], p['gla_ab'].reshape(1, GLA_KW), ng)


def _s5_kernel(rows, seg, carry_state, u_ref, hre_ref, him_ref, pre_ref, pim_ref, bre_ref, bim_ref,
               cre_ref, cim_ref, d_ref, ga_ref, gb_ref, y_ref, ore_ref, oim_ref, *scratch):
    u = u_ref[...]
    drive = _mm if carry_state else _mm3
    h_re = drive(u, bre_ref[...])
    h_im = drive(u, bim_ref[...])
    tpos = _iota((rows, 1), 0) % seg
    d = 1
    while d < seg:
        ar, ai = pre_ref[d - 1:d], pim_ref[d - 1:d]
        sr = jnp.where(tpos >= d, pltpu.roll(h_re, shift=d, axis=0), 0.0)
        sm = jnp.where(tpos >= d, pltpu.roll(h_im, shift=d, axis=0), 0.0)
        h_re, h_im = h_re + ar * sr - ai * sm, h_im + ar * sm + ai * sr
        d *= 2
    if carry_state:
        cre_s, cim_s = scratch

        @pl.when(pl.program_id(1) == 0)
        def _():
            cre_s[...] = hre_ref[0]
            cim_s[...] = him_ref[0]
        i_re, i_im = cre_s[...], cim_s[...]
        pr, pi = pre_ref[0:seg], pim_ref[0:seg]
        parts_re, parts_im = [], []
        for k in range(rows // seg):
            sl = slice(k * seg, (k + 1) * seg)
            p_re = h_re[sl] + pr * i_re - pi * i_im
            p_im = h_im[sl] + pr * i_im + pi * i_re
            i_re, i_im = p_re[seg - 1:seg], p_im[seg - 1:seg]
            parts_re.append(p_re)
            parts_im.append(p_im)
        h_re, h_im = jnp.concatenate(parts_re, axis=0), jnp.concatenate(parts_im, axis=0)
        cre_s[...] = i_re
        cim_s[...] = i_im
        ore_ref[0] = i_re
        oim_ref[0] = i_im
    else:
        i_re, i_im = hre_ref[...], him_ref[...]
        pr, pi = pre_ref[...], pim_ref[...]
        h_re, h_im = h_re + pr * i_re - pi * i_im, h_im + pr * i_im + pi * i_re
        ore_ref[...] = h_re
        oim_ref[...] = h_im
    y = _mm(h_re, cre_ref[...]) - _mm(h_im, cim_ref[...]) + d_ref[...] * u
    z = 0.5 * y * (1.0 + jnp.tanh(0.7978845608028654 * (y + 0.044715 * y * y * y)))
    y_ref[...] = _mm(z, ga_ref[...]) * _sigmoid(_mm(z, gb_ref[...]))


def s5_mixer(u, h0_re, h0_im, p, seq_len):
    n = u.shape[0]
    b = n // seq_len
    f32 = F32
    a_re, a_im = p['s5_a_re'], p['s5_a_im']
    dt = jnp.exp(p['s5_log_dt'])[:, None]
    mag = jnp.exp(a_re * dt)
    ab_re, ab_im = mag * jnp.cos(a_im * dt), mag * jnp.sin(a_im * dt)
    den = a_re * a_re + a_im * a_im
    cr = ((ab_re - 1.0) * a_re + ab_im * a_im) / den
    ci = (ab_im * a_re - (ab_re - 1.0) * a_im) / den
    bb_re = cr[..., None] * p['s5_b_re'] - ci[..., None] * p['s5_b_im']
    bb_im = cr[..., None] * p['s5_b_im'] + ci[..., None] * p['s5_b_re']
    eye_g = jnp.eye(S5_GROUPS, dtype=f32)

    def in_blockdiag(w):
        return jnp.einsum('gpi,gh->gihp', w, eye_g).reshape(S5_W, S5_STATE)

    def out_blockdiag(w):
        return jnp.einsum('gip,gh->gphi', w, eye_g).reshape(S5_STATE, S5_W)

    carry_state = seq_len > S5_TILE
    rows = S5_TILE
    seg = SUBLANE if carry_state else seq_len
    pw_re, pw_im = ab_re.reshape(1, S5_STATE), ab_im.reshape(1, S5_STATE)
    while pw_re.shape[0] < seg:
        tr, ti_ = pw_re[-1:], pw_im[-1:]
        pw_re, pw_im = (jnp.concatenate([pw_re, pw_re * tr - pw_im * ti_], axis=0),
                        jnp.concatenate([pw_im, pw_re * ti_ + pw_im * tr], axis=0))
    pw_re, pw_im = jnp.tile(pw_re, (rows // seg, 1)), jnp.tile(pw_im, (rows // seg, 1))
    weights = [pw_re, pw_im, in_blockdiag(bb_re), in_blockdiag(bb_im),
               out_blockdiag(p['s5_c_re']), out_blockdiag(p['s5_c_im']),
               p['s5_d'].reshape(1, S5_W), p['s5_glu_a'], p['s5_glu_b']]
    wspecs = [_full(w.shape) for w in weights]
    if carry_state:
        tiles = seq_len // rows
        grid = (b, tiles)
        tok = lambda w: pl.BlockSpec((rows, w), lambda i, j: (i * tiles + j, 0))
        st_in = pl.BlockSpec((1, 1, S5_STATE), lambda i, j: (i, 0, 0))
        ins = [u, h0_re.reshape(b, 1, S5_STATE), h0_im.reshape(b, 1, S5_STATE)]
        st_shape = jax.ShapeDtypeStruct((b, 1, S5_STATE), f32)
        scratch = [pltpu.VMEM((1, S5_STATE), f32)] * 2
        sem = ("parallel", "arbitrary")
        st_out = st_in
    else:
        grid = (n // rows,)
        tok = lambda w: pl.BlockSpec((rows, w), lambda i: (i, 0))
        st_in = tok(S5_STATE)
        ins = [u, jnp.repeat(h0_re, seq_len, axis=0), jnp.repeat(h0_im, seq_len, axis=0)]
        st_shape = jax.ShapeDtypeStruct((n, S5_STATE), f32)
        scratch = []
        sem = ("parallel",)
        st_out = st_in
    y, o_re, o_im = pl.pallas_call(
        functools.partial(_s5_kernel, rows, seg, carry_state),
        grid=grid,
        in_specs=[tok(S5_W), st_in, st_in] + wspecs,
        out_specs=[tok(S5_W), st_out, st_out],
        out_shape=[jax.ShapeDtypeStruct((n, S5_W), f32), st_shape, st_shape],
        scratch_shapes=scratch,
        compiler_params=_cparams(sem),
    )(*ins, *weights)
    if carry_state:
        return y, o_re.reshape(b, S5_STATE), o_im.reshape(b, S5_STATE)
    last = lambda h: h.reshape(b, seq_len, S5_STATE)[:, -1]
    return y, last(o_re), last(o_im)


def _rope_rotate(x, cos_t, sin_a, sin_b):
    w = x.shape[1]
    return x * cos_t + pltpu.roll(x, shift=w - MLA_ROPE // 2, axis=1) * sin_a + pltpu.roll(x, shift=MLA_ROPE // 2, axis=1) * sin_b


def _mla_keys(c, kr, cos_t, sin_a, sin_b, wuk_ref, pk_ref):
    k_nope = _mm(c, wuk_ref[...])
    k_rope = _rope_rotate(_mm3(kr, pk_ref[...]), cos_t, sin_a, sin_b)
    return k_nope, k_rope


def _mla_prep_kernel(pm_ref, cos_ref, sina_ref, sinb_ref, qag_ref, wuq_ref, qg_ref, kvag_ref,
                     wuk_ref, wuv_ref, pk_ref, q_ref, k_ref, v_ref, c_ref, kr_ref):
    pm = pm_ref[...]
    qd, kvd, kr = pm[:, 0:256], pm[:, 256:512], pm[:, 512:512 + MLA_ROPE]
    tile4 = lambda t: jnp.concatenate([t] * MLA_HEADS, axis=1)
    cos_t, sin_a, sin_b = tile4(cos_ref[...]), tile4(sina_ref[...]), tile4(sinb_ref[...])
    e128 = _block_ones(MLA_HW, MLA_HP)
    qn = qd * lax.rsqrt(jnp.mean(qd * qd, axis=-1, keepdims=True) + EPS) * qag_ref[...]
    q = _mm(qn, wuq_ref[...])
    q = q * lax.rsqrt(_mm2(q * q, e128) * (1.0 / MLA_QK) + EPS) * qg_ref[...]
    q = _rope_rotate(q, cos_t, sin_a, sin_b)
    c = kvd * lax.rsqrt(jnp.mean(kvd * kvd, axis=-1, keepdims=True) + EPS) * kvag_ref[...]
    c_ref[...] = c
    kr_ref[...] = kr
    k_nope, k_rope = _mla_keys(c, kr, cos_t, sin_a, sin_b, wuk_ref, pk_ref)
    ssq = _mm2(k_nope * k_nope, e128) + jnp.sum(kr * kr, axis=-1, keepdims=True)
    kfull = (k_nope + k_rope) * (lax.rsqrt(ssq * (1.0 / MLA_QK) + EPS) * MLA_SCALE)
    vfull = _mm(c, wuv_ref[...])
    for h in range(MLA_HEADS):
        hs = slice(h * MLA_HP, (h + 1) * MLA_HP)
        q_ref[h] = _bf(q[:, hs])
        k_ref[h] = _bf(kfull[:, hs])
        v_ref[h] = _bf(vfull[:, hs])


def _flash_kernel(t, q_ref, k_ref, v_ref, o_ref):
    iq = pl.program_id(2)
    q = q_ref[0]

    def step(ik, carry, mask):
        m, l, acc = carry
        off = pl.multiple_of(ik * t, t)
        s = _dg(q, k_ref[0, pl.ds(off, t), :], 1, 1)
        if mask is not None:
            s = jnp.where(mask, s, NEG_INF)
        m_new = jnp.maximum(m, jnp.max(s, axis=-1, keepdims=True))
        alpha = jnp.exp(m - m_new)
        pexp = jnp.exp(s - m_new)
        l = alpha * l + jnp.sum(pexp, axis=-1, keepdims=True)
        acc = alpha * acc + jnp.dot(_bf(pexp), v_ref[0, pl.ds(off, t), :], preferred_element_type=F32)
        return m_new, l, acc

    init = (jnp.full((t, 1), NEG_INF, F32), jnp.zeros((t, 1), F32), jnp.zeros((t, MLA_HP), F32))
    carry = lax.fori_loop(0, iq, lambda ik, cr: step(ik, cr, None), init)
    m, l, acc = step(iq, carry, _iota((t, t), 1) <= _iota((t, t), 0))
    o_ref[...] = acc / l


def mla_prompt(pm, p, batch, seq_len, tables):
    n = pm.shape[0]
    rows = min(ROW_TILE, seq_len)
    tiles = seq_len // rows
    cos_t, sin_a, sin_b = tables
    tok = lambda w: pl.BlockSpec((rows, w), lambda i: (i, 0))
    tab = pl.BlockSpec((rows, MLA_HP), lambda i: (i % tiles, 0))
    hd = pl.BlockSpec((MLA_HEADS, rows, MLA_HP), lambda i: (0, i, 0))
    weights = [p['qag'], p['wuq_pad'], p['qg_pad'], p['kvag'], p['wuk_pad'], p['wuv_pad'], p['pk']]
    hshape = jax.ShapeDtypeStruct((MLA_HEADS, n, MLA_HP), BF16)
    q, k, v, c, kr = pl.pallas_call(
        _mla_prep_kernel,
        grid=(n // rows,),
        in_specs=[tok(MLA_PAD), tab, tab, tab] + [_full(w.shape) for w in weights],
        out_specs=[hd, hd, hd, tok(MLA_LORA), tok(MLA_ROPE)],
        out_shape=[hshape, hshape, hshape, jax.ShapeDtypeStruct((n, MLA_LORA), F32),
                   jax.ShapeDtypeStruct((n, MLA_ROPE), F32)],
        compiler_params=_cparams(("parallel",)),
    )(pm, cos_t, sin_a, sin_b, *weights)
    tq = min(ATT_T, seq_len)
    nq = seq_len // tq
    kv_spec = pl.BlockSpec((1, seq_len, MLA_HP), lambda b, h, i: (h, b, 0))
    o = pl.pallas_call(
        functools.partial(_flash_kernel, tq),
        grid=(batch, MLA_HEADS, nq),
        in_specs=[pl.BlockSpec((1, tq, MLA_HP), lambda b, h, i: (h, b * nq + i, 0)), kv_spec, kv_spec],
        out_specs=pl.BlockSpec((tq, MLA_HP), lambda b, h, i: (b * nq + i, h)),
        out_shape=jax.ShapeDtypeStruct((n, MLA_HW), F32),
        compiler_params=_cparams(("parallel", "parallel", "arbitrary")),
    )(q, k, v)
    return o, c, kr


def _mla_decode_kernel(n_pages, t_new, *refs):
    (pm_ref, cosn_ref, sinan_ref, sinbn_ref, cosp_ref, sinp_ref,
     qag_ref, wuq_ref, qg_ref, kvag_ref, wukp_ref, wuv_ref, pk_ref, wuk_ref, kgain_ref) = refs[1:16]
    c_refs = refs[16:16 + n_pages]
    kr_refs = refs[16 + n_pages:16 + 2 * n_pages]
    o_ref, c_out, kr_out, q_s, qn_s, qr_s, m_s, l_s, acc_s, cbf_s = refs[16 + 2 * n_pages:]
    j = pl.program_id(1)
    nq = MLA_HEADS * SUBLANE
    n = n_pages * PAGE_SIZE
    half = MLA_ROPE // 2
    tile4 = lambda t: jnp.concatenate([t] * MLA_HEADS, axis=1)
    row_head = (_iota((nq, MLA_HW), 0) // SUBLANE == _iota((nq, MLA_HW), 1) // MLA_HP)
    head_sel = (_iota((nq, MLA_LORA), 0) // SUBLANE == _iota((nq, MLA_LORA), 1) // MLA_NOPE)

    def softmax_step(s, values):
        m_old = m_s[...]
        m_new = jnp.maximum(m_old, jnp.max(s, axis=-1, keepdims=True))
        alpha = jnp.exp(m_old - m_new)
        pexp = jnp.exp(s - m_new)
        l_s[...] = alpha * l_s[...] + jnp.sum(pexp, axis=-1, keepdims=True)
        m_s[...] = m_new
        acc_s[...] = acc_s[...] * alpha + _mm(pexp, values)

    pm = pm_ref[0]
    qd, kvd, kr_new = pm[:, 0:256], pm[:, 256:512], pm[:, 512:512 + MLA_ROPE]
    c_new = kvd * lax.rsqrt(jnp.mean(kvd * kvd, axis=-1, keepdims=True) + EPS) * kvag_ref[...]
    cos_n, sin_an, sin_bn = tile4(cosn_ref[...]), tile4(sinan_ref[...]), tile4(sinbn_ref[...])

    @pl.when(j == 0)
    def _():
        e128 = _block_ones(MLA_HW, MLA_HP)
        qn = qd * lax.rsqrt(jnp.mean(qd * qd, axis=-1, keepdims=True) + EPS) * qag_ref[...]
        q = _mm(qn, wuq_ref[...])
        q = q * lax.rsqrt(_mm2(q * q, e128) * (1.0 / MLA_QK) + EPS) * qg_ref[...]
        q = _rope_rotate(q, cos_n, sin_an, sin_bn)
        q4 = jnp.concatenate([q] * MLA_HEADS, axis=0)
        q_s[...] = jnp.where(row_head, q4, 0.0)
        li, lo = _iota((MLA_HW, MLA_LORA), 0), _iota((MLA_HW, MLA_LORA), 1)
        take_nope = ((li // MLA_HP == lo // MLA_NOPE) & (li % MLA_HP == lo % MLA_NOPE)).astype(BF16)
        qn_s[...] = _bf(jnp.where(head_sel, _mm(q4, take_nope), 0.0))
        ri, ro = _iota((MLA_HW, MLA_ROPE), 0), _iota((MLA_HW, MLA_ROPE), 1)
        take_rope = (ri % MLA_HP == MLA_NOPE + ro).astype(BF16)
        qr_s[...] = _bf(_mm(jnp.where(row_head, q4, 0.0), take_rope))
        m_s[...] = jnp.full(m_s.shape, NEG_INF, F32)
        l_s[...] = jnp.zeros(l_s.shape, F32)
        acc_s[...] = jnp.zeros(acc_s.shape, F32)
        c_out[0] = c_new
        kr_out[0] = kr_new

    for g in range(n_pages):
        cbf_s[g * PAGE_SIZE:(g + 1) * PAGE_SIZE, :] = _bf(c_refs[g][0, 0])
    kt = jnp.concatenate([kr_refs[g][0, 0] for g in range(n_pages)], axis=1)
    sq_rope = jnp.sum(kt * kt, axis=0, keepdims=True)
    kg = kt * kgain_ref[...]
    x1, x2 = kg[0:half], kg[half:]
    cos_p, sin_p = cosp_ref[...], sinp_ref[...]
    krot = jnp.concatenate([x1 * cos_p - x2 * sin_p, x2 * cos_p + x1 * sin_p], axis=0)
    s_rope = jnp.dot(qr_s[...], _bf(krot), preferred_element_type=F32)
    nt = n // DEC_SUB
    sel_b = _bf(head_sel.astype(F32))
    lanes = lambda t, i: t[:, i * DEC_SUB:(i + 1) * DEC_SUB]
    val = [dict() for _ in range(nt)]

    def stage(k, i):
        d = val[i]
        if k == 0:
            d['cb'] = cbf_s[i * DEC_SUB:(i + 1) * DEC_SUB, :]
            d['kpre'] = jnp.dot(d['cb'], wuk_ref[...], preferred_element_type=F32)
        elif k == 1:
            d['ksq'], d['kb'] = _bf(d['kpre'] * d['kpre']), _bf(d['kpre'])
        elif k == 2:
            d['ssq'] = _dg(sel_b, d['ksq'], 1, 1)
            d['sc'] = _dg(qn_s[...], d['kb'], 1, 1)
        elif k == 3:
            norm = lax.rsqrt((d['ssq'] + lanes(sq_rope, i)) * (1.0 / MLA_QK) + EPS) * MLA_SCALE
            d['s'] = (d['sc'] + lanes(s_rope, i)) * norm
            d['m'] = jnp.max(d['s'], axis=-1, keepdims=True)
        elif k == 4:
            pexp = jnp.exp(d['s'] - d['m'])
            d['l'] = jnp.sum(pexp, axis=-1, keepdims=True)
            d['acc'] = jnp.dot(_bf(pexp), d['cb'], preferred_element_type=F32)

    n_stage = 5
    for t in range(nt + n_stage - 1):
        for k in range(n_stage):
            if 0 <= t - k < nt:
                stage(k, t - k)
    m_old = m_s[...]
    m_new = m_old
    for d in val:
        m_new = jnp.maximum(m_new, d['m'])
    alpha = jnp.exp(m_old - m_new)
    l_new = alpha * l_s[...]
    acc = acc_s[...] * alpha
    for d in val:
        w = jnp.exp(d['m'] - m_new)
        l_new = l_new + w * d['l']
        acc = acc + w * d['acc']
    m_s[...] = m_new
    l_s[...] = l_new
    acc_s[...] = acc

    @pl.when(j == pl.num_programs(1) - 1)
    def _():
        k_nope, k_rope = _mla_keys(c_new, kr_new, cos_n, sin_an, sin_bn, wukp_ref, pk_ref)
        ssq_n = _mm2l(row_head.astype(BF16), k_nope * k_nope, 1, 1)
        ssq_n = ssq_n + _mm2l(jnp.ones((SUBLANE, MLA_ROPE), BF16), kr_new * kr_new, 1, 1)[0:1]
        s_n = _mm(q_s[...], k_nope + k_rope, 1, 1) * (lax.rsqrt(ssq_n * (1.0 / MLA_QK) + EPS) * MLA_SCALE)
        qry_t = _iota((nq, SUBLANE), 0) % SUBLANE
        key_t = _iota((nq, SUBLANE), 1)
        softmax_step(jnp.where((key_t <= qry_t) & (key_t < t_new), s_n, NEG_INF), c_new)
        o_lat = acc_s[...] / l_s[...]
        full = jnp.where(row_head, _mm(o_lat, wuv_ref[...]), 0.0)
        gather = (_iota((SUBLANE, nq), 0) == _iota((SUBLANE, nq), 1) % SUBLANE).astype(BF16)
        o_ref[0] = _mm2l(gather, full)


def mla_decode(pm, cache_c, cache_krt, page_table, layer, p, tables_new, tables_past_t, t_new):
    b = pm.shape[0]
    pages = page_table.shape[1]
    steps = pages // DEC_PAGES
    n = DEC_PAGES * PAGE_SIZE
    per_b = lambda w: pl.BlockSpec((1, SUBLANE, w), lambda i, j, pt: (i, 0, 0))
    tab_new = pl.BlockSpec((SUBLANE, MLA_HP), lambda i, j, pt: (0, 0))
    tab_past = pl.BlockSpec((MLA_ROPE // 2, n), lambda i, j, pt: (0, j))
    weights = [p['qag'], p['wuq_pad'], p['qg_pad'], p['kvag'], p['wuk_pad'], p['wuv_pad'], p['pk'],
               p['wuk'], p['kgain_col']]
    wspecs = [pl.BlockSpec(w.shape, lambda i, j, pt, nd=w.ndim: (0,) * nd) for w in weights]

    def page_spec(rows, w, slot):
        return pl.BlockSpec((1, 1, rows, w), lambda i, j, pt: (layer, pt[i, j * DEC_PAGES + slot], 0, 0))

    c_specs = [page_spec(PAGE_SIZE, MLA_LORA, s) for s in range(DEC_PAGES)]
    kr_specs = [page_spec(MLA_ROPE, PAGE_SIZE, s) for s in range(DEC_PAGES)]
    nq = MLA_HEADS * SUBLANE
    grid_spec = pltpu.PrefetchScalarGridSpec(
        num_scalar_prefetch=1,
        grid=(b, steps),
        in_specs=[per_b(MLA_PAD), tab_new, tab_new, tab_new, tab_past, tab_past]
        + wspecs + c_specs + kr_specs,
        out_specs=[per_b(MLA_HW), per_b(MLA_LORA), per_b(MLA_ROPE)],
        scratch_shapes=[pltpu.VMEM((nq, MLA_HW), F32), pltpu.VMEM((nq, MLA_LORA), BF16),
                        pltpu.VMEM((nq, MLA_ROPE), BF16), pltpu.VMEM((nq, 1), F32),
                        pltpu.VMEM((nq, 1), F32), pltpu.VMEM((nq, MLA_LORA), F32),
                        pltpu.VMEM((n, MLA_LORA), BF16)],
    )
    o, c, kr = pl.pallas_call(
        functools.partial(_mla_decode_kernel, DEC_PAGES, t_new),
        grid_spec=grid_spec,
        out_shape=[jax.ShapeDtypeStruct((b, SUBLANE, MLA_HW), F32),
                   jax.ShapeDtypeStruct((b, SUBLANE, MLA_LORA), F32),
                   jax.ShapeDtypeStruct((b, SUBLANE, MLA_ROPE), F32)],
        compiler_params=_cparams(("parallel", "arbitrary")),
    )(page_table, pm, *tables_new, *tables_past_t, *weights,
      *([cache_c] * DEC_PAGES), *([cache_krt] * DEC_PAGES))
    return o, c, kr


def rope_tables_transposed(pos):
    half = MLA_ROPE // 2
    inv = ROPE_THETA ** (-jnp.arange(half, dtype=F32) / half)
    ang = inv[:, None] * pos.astype(F32)[None, :]
    return jnp.cos(ang), jnp.sin(ang)


def rope_tables(pos):
    half = MLA_ROPE // 2
    inv = ROPE_THETA ** (-jnp.arange(half, dtype=F32) / half)
    ang = pos.astype(F32)[:, None] * inv
    cos, sin = jnp.cos(ang), jnp.sin(ang)
    n = pos.shape[0]
    ones, zeros = jnp.ones((n, MLA_NOPE), F32), jnp.zeros((n, MLA_NOPE), F32)
    zpad = jnp.zeros((n, MLA_HP - MLA_QK), F32)
    zh = jnp.zeros((n, half), F32)
    cos_t = jnp.concatenate([ones, cos, cos, zpad + 1.0], axis=1)
    sin_a = jnp.concatenate([zeros, -sin, zh, zpad], axis=1)
    sin_b = jnp.concatenate([zeros, zh, sin, zpad], axis=1)
    return cos_t, sin_a, sin_b


def _merge_kernel(x_ref, xn_ref, yrw_ref, ygla_ref, ys5_ref, ymla_ref, wg_ref, wb_ref, wbm_ref, wo_ref, o_ref):
    xb = _bf(xn_ref[...])
    merged = None
    for n, y_ref in enumerate((yrw_ref, ygla_ref, ys5_ref, ymla_ref)):
        gate = _sigmoid(jnp.dot(xb, wg_ref[:, n * D_MODEL:(n + 1) * D_MODEL], preferred_element_type=F32))
        w = wbm_ref[...] if n == 3 else wb_ref[n]
        term = gate * jnp.dot(_bf(y_ref[...]), w, preferred_element_type=F32)
        merged = term if merged is None else merged + term
    o_ref[...] = x_ref[...] + jnp.dot(_bf(merged), wo_ref[...], preferred_element_type=F32)


def merge(x, xn, y_rw, y_gla, y_s5, y_mla, p):
    n = x.shape[0]
    row = lambda w: pl.BlockSpec((ROW_TILE, w), lambda i: (i, 0))
    weights = [p['w_gate'], p['w_branch3'], p['w_branch_mla'], p['w_out']]
    return pl.pallas_call(
        _merge_kernel,
        grid=(n // ROW_TILE,),
        in_specs=[row(D_MODEL), row(D_MODEL), row(BRANCH_W), row(BRANCH_W), row(BRANCH_W), row(MLA_HW)]
        + [_full(w.shape) for w in weights],
        out_specs=row(D_MODEL),
        out_shape=jax.ShapeDtypeStruct((n, D_MODEL), F32),
        compiler_params=_cparams(("parallel",)),
    )(x, xn, y_rw, y_gla, y_s5, y_mla, *weights)


def _mlp_kernel(ff_chunk, x_ref, g_ref, w1_ref, w2_ref, o_ref):
    x = x_ref[...]
    xb = _bf(x * lax.rsqrt(jnp.mean(x * x, axis=-1, keepdims=True) + EPS) * g_ref[...])
    acc = x
    for o in range(0, D_FF, ff_chunk):
        h = jnp.maximum(jnp.dot(xb, w1_ref[:, o:o + ff_chunk], preferred_element_type=F32), 0.0)
        acc = acc + jnp.dot(_bf(h * h), w2_ref[o:o + ff_chunk, :], preferred_element_type=F32)
    o_ref[...] = acc


def mlp(x, p):
    n = x.shape[0]
    row = pl.BlockSpec((ROW_TILE, D_MODEL), lambda i: (i, 0))
    return pl.pallas_call(
        functools.partial(_mlp_kernel, 1024),
        grid=(n // ROW_TILE,),
        in_specs=[row, _full((1, D_MODEL)), _full((D_MODEL, D_FF)), _full((D_FF, D_MODEL))],
        out_specs=row,
        out_shape=jax.ShapeDtypeStruct((n, D_MODEL), F32),
        compiler_params=_cparams(("parallel",)),
    )(x, p['norm2_g'].reshape(1, D_MODEL), p['w_ff1'], p['w_ff2'])


def _pad_cols(w, width):
    return jnp.pad(w, ((0, 0), (0, width - w.shape[1])))


def _head_pad(w, used):
    kdim = w.shape[0]
    w = w.reshape(kdim, MLA_HEADS, used)
    return jnp.pad(w, ((0, 0), (0, 0), (0, MLA_HP - used))).reshape(kdim, MLA_HW)


def prepare_layer(l, W):
    p = {}
    w_in = W['w_in'][l]
    o_gla, o_s5, o_mla = RW_COLS, RW_COLS + GLA_COLS, RW_COLS + GLA_COLS + S5_W
    w_gla = w_in[:, o_gla:o_s5]
    w_gla = jnp.concatenate([w_gla[:, 0:512], w_gla[:, 528:784], w_gla[:, 512:528]], axis=1)
    p['w_rw'] = _bf(w_in[:, :RW_COLS])
    p['w_all'] = _bf(jnp.concatenate([w_in[:, :RW_COLS], _pad_cols(w_gla, GLA_PAD), w_in[:, o_s5:o_mla],
                                      _pad_cols(w_in[:, o_mla:], MLA_PAD)], axis=1))
    for name in ('norm1_g', 'rw_mu', 'rw_w0', 'rw_w2', 'rw_a0', 'rw_a2', 'rw_g2', 'rw_k_k', 'rw_k_a',
                 'rw_ln_g', 'rw_ln_b', 'gla_a2', 'gla_ab', 'gla_norm_g', 's5_a_re', 's5_a_im', 's5_log_dt',
                 's5_b_re', 's5_b_im', 's5_c_re', 's5_c_im', 's5_glu_a', 's5_glu_b', 'norm2_g'):
        p[name] = W[name][l]
    p['rw_r_k'] = W['rw_r_k'][l].reshape(RW_W)
    p['s5_d'] = W['s5_d'][l].reshape(S5_W)
    if l > 0:
        for name in ('rw_v0', 'rw_v1', 'rw_v2'):
            p[name] = W[name][l - 1]
    k_g = W['mla_k_g'][l]
    p['qag'] = W['mla_qa_g'][l].reshape(1, MLA_LORA)
    p['kvag'] = W['mla_kva_g'][l].reshape(1, MLA_LORA)
    p['wuq_pad'] = _head_pad(W['mla_w_uq'][l], MLA_QK)
    qg = W['mla_q_g'][l] * jnp.concatenate([k_g[:MLA_NOPE], jnp.ones((MLA_ROPE,), F32)])
    p['qg_pad'] = jnp.tile(jnp.pad(qg, (0, MLA_HP - MLA_QK)), MLA_HEADS).reshape(1, MLA_HW)
    p['wuk_pad'] = _head_pad(W['mla_w_uk'][l].reshape(MLA_LORA, MLA_HEADS * MLA_NOPE), MLA_NOPE)
    wuv = W['mla_w_uv'][l].reshape(MLA_LORA, MLA_HEADS * MLA_V)
    p['wuv_pad'] = _head_pad(wuv, MLA_V)
    p['wuk'] = _bf(W['mla_w_uk'][l].reshape(MLA_LORA, MLA_HEADS * MLA_NOPE))
    p['kgain_col'] = k_g[MLA_NOPE:].reshape(MLA_ROPE, 1)
    place = jnp.pad(jnp.diag(k_g[MLA_NOPE:]), ((0, 0), (MLA_NOPE, MLA_HP - MLA_QK)))
    p['pk'] = jnp.tile(place, (1, MLA_HEADS))
    p['w_gate'] = _bf(W['w_gate'][l])
    wb = W['w_branch'][l]
    p['w_branch3'] = _bf(wb)
    wbm = wb[3].reshape(MLA_HEADS, MLA_V, D_MODEL)
    p['w_branch_mla'] = _bf(jnp.pad(wbm, ((0, 0), (0, MLA_HP - MLA_V), (0, 0))).reshape(MLA_HW, D_MODEL))
    p['w_out'] = _bf(W['w_out'][l])
    p['w_ff1'] = _bf(W['w_ff1'][l])
    p['w_ff2'] = _bf(W['w_ff2'][l])
    return p


def _gla_state_in(s):
    eye = jnp.eye(GLA_HEADS, dtype=F32)
    return jnp.einsum('bhdv,hg->bhvgd', s, eye).reshape(s.shape[0], GLA_W, GLA_KW)


def _gla_state_out(st):
    b = st.shape[0]
    st = st.reshape(b, GLA_HEADS, GLA_DV, GLA_HEADS, GLA_DK)
    return jnp.stack([st[:, h, :, h, :] for h in range(GLA_HEADS)], axis=1).swapaxes(2, 3)


def kernel(x_prompt, x_sample, cache_mla_latent, cache_mla_krope, page_table, state_rwkv, state_shift, state_gla, state_s5_re, state_s5_im, norm1_g, w_in, rw_mu, rw_w0, rw_w2, rw_a0, rw_a2, rw_g2, rw_v0, rw_v1, rw_v2, rw_k_k, rw_k_a, rw_r_k, rw_ln_g, rw_ln_b, gla_a2, gla_ab, gla_norm_g, s5_a_re, s5_a_im, s5_log_dt, s5_b_re, s5_b_im, s5_c_re, s5_c_im, s5_d, s5_glu_a, s5_glu_b, mla_qa_g, mla_w_uq, mla_kva_g, mla_w_uk, mla_w_uv, mla_q_g, mla_k_g, w_gate, w_branch, w_out, norm2_g, w_ff1, w_ff2):
    W = dict(norm1_g=norm1_g, w_in=w_in, rw_mu=rw_mu, rw_w0=rw_w0, rw_w2=rw_w2, rw_a0=rw_a0, rw_a2=rw_a2,
             rw_g2=rw_g2, rw_v0=rw_v0, rw_v1=rw_v1, rw_v2=rw_v2, rw_k_k=rw_k_k, rw_k_a=rw_k_a, rw_r_k=rw_r_k,
             rw_ln_g=rw_ln_g, rw_ln_b=rw_ln_b, gla_a2=gla_a2, gla_ab=gla_ab, gla_norm_g=gla_norm_g,
             s5_a_re=s5_a_re, s5_a_im=s5_a_im, s5_log_dt=s5_log_dt, s5_b_re=s5_b_re, s5_b_im=s5_b_im,
             s5_c_re=s5_c_re, s5_c_im=s5_c_im, s5_d=s5_d, s5_glu_a=s5_glu_a, s5_glu_b=s5_glu_b,
             mla_qa_g=mla_qa_g, mla_w_uq=mla_w_uq, mla_kva_g=mla_kva_g, mla_w_uk=mla_w_uk, mla_w_uv=mla_w_uv,
             mla_q_g=mla_q_g, mla_k_g=mla_k_g, w_gate=w_gate, w_branch=w_branch, w_out=w_out,
             norm2_g=norm2_g, w_ff1=w_ff1, w_ff2=w_ff2)
    bp, tp, _ = x_prompt.shape
    bs, ts, _ = x_sample.shape
    depth = w_in.shape[0]
    np_, ns = bp * tp, bs * ts
    past_len = page_table.shape[1] * PAGE_SIZE
    tpad = SUBLANE

    tables_p = rope_tables(jnp.arange(tp, dtype=jnp.int32))
    tables_new = rope_tables(past_len + jnp.arange(tpad, dtype=jnp.int32))
    tables_past = rope_tables_transposed(jnp.arange(past_len, dtype=jnp.int32))
    cache_krt = jnp.swapaxes(cache_mla_krope, 2, 3)

    xp, xs = x_prompt.reshape(np_, D_MODEL), x_sample.reshape(ns, D_MODEL)
    pad_t = lambda a: jnp.pad(a.reshape(bs, ts, a.shape[-1]), ((0, 0), (0, tpad - ts), (0, 0)))
    unpad = lambda a: a[:, :ts].reshape(ns, a.shape[-1])
    vf_p = vf_s = None
    outs = {k: [] for k in ('c_p', 'kr_p', 'c_s', 'kr_s', 'rw_p', 'rw_s', 'sh_p', 'sh_s', 'gla_p', 'gla_s',
                            're_p', 'im_p', 're_s', 'im_s')}
    zeros_p = lambda *shape: jnp.zeros((bp,) + shape, F32)
    for l in range(depth):
        p = prepare_layer(l, W)
        xn_p, prw_p, pgla_p, ps5_p, pmla_p = norm_proj(xp, p['norm1_g'], p['w_all'])
        xn_s, prw_s, pgla_s, ps5_s, pmla_s = norm_proj(xs, p['norm1_g'], p['w_all'])
        outs['sh_p'].append(xn_p.reshape(bp, tp, D_MODEL)[:, -1])
        outs['sh_s'].append(xn_s.reshape(bs, ts, D_MODEL)[:, -1])

        y_rw_p, vf_p, s_rw_p = rwkv_mixer(prw_p.reshape(bp, tp, RW_COLS), zeros_p(1, RW_COLS),
                                          zeros_p(RW_HEADS, RW_HD, RW_HD), vf_p, p, tp)
        shift_proj = small_matmul(state_shift[l], p['w_rw']).reshape(bs, 1, RW_COLS)
        y_rw_s, vf_s, s_rw_s = rwkv_mixer(pad_t(prw_s), shift_proj, state_rwkv[l], vf_s, p, ts)
        outs['rw_p'].append(s_rw_p)
        outs['rw_s'].append(s_rw_s)

        y_gla_p, s_gla_p = gla_mixer(pgla_p.reshape(bp, tp, GLA_PAD), zeros_p(GLA_W, GLA_KW), p, tp)
        y_gla_s, s_gla_s = gla_mixer(pad_t(pgla_s), _gla_state_in(state_gla[l]), p, ts)
        outs['gla_p'].append(_gla_state_out(s_gla_p))
        outs['gla_s'].append(_gla_state_out(s_gla_s))

        y_s5_p, re_p, im_p = s5_mixer(ps5_p, zeros_p(S5_STATE), zeros_p(S5_STATE), p, tp)
        y_s5_s, re_s, im_s = s5_mixer(ps5_s, state_s5_re[l].reshape(bs, S5_STATE),
                                      state_s5_im[l].reshape(bs, S5_STATE), p, ts)
        for key, val, bb in (('re_p', re_p, bp), ('im_p', im_p, bp), ('re_s', re_s, bs), ('im_s', im_s, bs)):
            outs[key].append(val.reshape(bb, S5_GROUPS, S5_P))

        y_mla_p, c_p, kr_p = mla_prompt(pmla_p, p, bp, tp, tables_p)
        y_mla_s, c_s, kr_s = mla_decode(pad_t(pmla_s), cache_mla_latent, cache_krt, page_table, l, p,
                                        tables_new, tables_past, ts)
        outs['c_p'].append(c_p.reshape(bp, tp, MLA_LORA))
        outs['kr_p'].append(kr_p.reshape(bp, tp, MLA_ROPE))
        outs['c_s'].append(c_s[:, :ts])
        outs['kr_s'].append(kr_s[:, :ts])

        xp = mlp(merge(xp, xn_p, y_rw_p.reshape(np_, RW_W), y_gla_p.reshape(np_, GLA_W), y_s5_p, y_mla_p, p), p)
        xs = mlp(merge(xs, xn_s, unpad(y_rw_s), unpad(y_gla_s), y_s5_s, unpad(y_mla_s), p), p)

    st = lambda k: jnp.stack(outs[k], axis=0)
    return (xp.reshape(bp, tp, D_MODEL), xs.reshape(bs, ts, D_MODEL),
            st('c_p'), st('kr_p'), st('c_s'), st('kr_s'),
            st('rw_p'), st('rw_s'), st('sh_p'), st('sh_s'),
            st('gla_p'), st('gla_s'), st('re_p'), st('im_p'), st('re_s'), st('im_s'))
```

```python
import functools

import jax
import jax.numpy as jnp
from jax import lax
from jax.experimental import pallas as pl
from jax.experimental.pallas import tpu as pltpu

F32 = jnp.float32
BF16 = jnp.bfloat16

D_MODEL = 1024
N_BRANCH = 4
BRANCH_W = 256
EPS = 1e-6
PAGE_SIZE = 128
RW_HEADS, RW_HD, RW_W = 4, 64, 256
RW_COLS = 896
RW_GN_EPS = 64e-5
GLA_HEADS, GLA_DK, GLA_DV, GLA_W = 4, 32, 64, 256
GLA_KW = GLA_HEADS * GLA_DK
GLA_A_LORA = 16
GLA_TAU = 16.0
GLA_COLS = 784
GLA_PAD = 896
S5_GROUP, S5_GROUPS, S5_P, S5_W = 16, 16, 64, 256
S5_STATE = S5_GROUPS * S5_P
MLA_HEADS, MLA_NOPE, MLA_ROPE, MLA_QK, MLA_V = 4, 64, 32, 96, 64
MLA_LORA = 256
MLA_COLS = 544
MLA_PAD = 640
MLA_HP = 128
MLA_HW = MLA_HEADS * MLA_HP
MLA_SCALE = MLA_QK ** -0.5
ROPE_THETA = 10000.0
NEG_INF = -1e30
D_FF = 4096
IN_PAD = RW_COLS + GLA_PAD + S5_W + MLA_PAD

V7X_VMEM_BYTES = 64 * 1024 * 1024
VMEM_LIMIT = V7X_VMEM_BYTES - 8 * 1024 * 1024
LANE = 128
SUBLANE = 8

ROW_TILE = 512
RW_CHUNK = 64
RW_BATCH = 4
GLA_SUB = 16
GLA_BATCH = 4
S5_TILE = 128
ATT_TQ = 1024
ATT_TK = 512
DEC_PAGES = 32
DEC_SUB = 512


def _cparams(sem):
    return pltpu.CompilerParams(dimension_semantics=sem, vmem_limit_bytes=VMEM_LIMIT)


def _bf(x):
    return x.astype(BF16)


def _dg(a, b, ca, cb):
    return lax.dot_general(a, b, (((ca,), (cb,)), ((), ())), preferred_element_type=F32)


def _mm(a, b, ca=1, cb=0):
    return _dg(_bf(a), _bf(b), ca, cb)


def _split(x):
    hi = _bf(x)
    return hi, _bf(x - hi.astype(F32))


def _mm3(a, b, ca=1, cb=0):
    ah, al = _split(a)
    bh, bl = _split(b)
    return _dg(ah, bh, ca, cb) + _dg(al, bh, ca, cb) + _dg(ah, bl, ca, cb)


def _mm2(a, b01, ca=1, cb=0):
    ah, al = _split(a)
    return _dg(ah, b01, ca, cb) + _dg(al, b01, ca, cb)


def _mm2l(a01, b, ca=1, cb=0):
    bh, bl = _split(b)
    return _dg(a01, bh, ca, cb) + _dg(a01, bl, ca, cb)


def _sigmoid(x):
    return 1.0 / (1.0 + jnp.exp(-x))


def _softplus(x):
    return jnp.maximum(x, 0.0) + jnp.log(1.0 + jnp.exp(-jnp.abs(x)))


def _iota(shape, dim):
    return lax.broadcasted_iota(jnp.int32, shape, dim)


def _block_ones(n, blk, dtype=BF16):
    return (_iota((n, n), 0) // blk == _iota((n, n), 1) // blk).astype(dtype)


def _full(shape):
    nd = len(shape)
    return pl.BlockSpec(shape, lambda *_: (0,) * nd)


def _norm_proj_kernel(x_ref, g_ref, w_ref, xn_ref, rw_ref, gla_ref, s5_ref, mla_ref):
    x = x_ref[...]
    xn = x * lax.rsqrt(jnp.mean(x * x, axis=-1, keepdims=True) + EPS) * g_ref[...]
    xn_ref[...] = xn
    xb = _bf(xn)
    o = 0
    for ref in (rw_ref, gla_ref, s5_ref, mla_ref):
        w = ref.shape[1]
        ref[...] = jnp.dot(xb, w_ref[:, o:o + w], preferred_element_type=F32)
        o += w


def norm_proj(x, g, w_all):
    n = x.shape[0]
    widths = (RW_COLS, GLA_PAD, S5_W, MLA_PAD)
    row = lambda w: pl.BlockSpec((ROW_TILE, w), lambda i: (i, 0))
    return pl.pallas_call(
        _norm_proj_kernel,
        grid=(n // ROW_TILE,),
        in_specs=[row(D_MODEL), _full((1, D_MODEL)), _full((D_MODEL, IN_PAD))],
        out_specs=[row(D_MODEL)] + [row(w) for w in widths],
        out_shape=[jax.ShapeDtypeStruct((n, D_MODEL), F32)]
        + [jax.ShapeDtypeStruct((n, w), F32) for w in widths],
        compiler_params=_cparams(("parallel",)),
    )(x, g.reshape(1, D_MODEL), w_all)


def _matmul_kernel(a_ref, b_ref, o_ref):
    o_ref[...] = _mm(a_ref[...], b_ref[...])


def small_matmul(a, b):
    m, n = a.shape[0], b.shape[1]
    return pl.pallas_call(
        _matmul_kernel,
        out_shape=jax.ShapeDtypeStruct((m, n), F32),
        compiler_params=_cparams(None),
    )(a, b)


def _rwkv_kernel(has_vres, nb, c, t_valid, *refs):
    if has_vres:
        (prw_ref, sp_ref, s0_ref, vf_ref, mu_ref, w0_ref, w2_ref, a0_ref, a2_ref, g2_ref,
         v0_ref, v1_ref, v2_ref, kk_ref, ka_ref, rk_ref, lng_ref, lnb_ref,
         y_ref, s_ref, carry_ref) = refs
    else:
        (prw_ref, sp_ref, s0_ref, mu_ref, w0_ref, w2_ref, a0_ref, a2_ref, g2_ref,
         kk_ref, ka_ref, rk_ref, lng_ref, lnb_ref,
         y_ref, vout_ref, s_ref, carry_ref) = refs

    @pl.when(pl.program_id(1) == 0)
    def _():
        s_ref[...] = s0_ref[...]
        carry_ref[...] = sp_ref[...]

    n = nb * c
    rowc = _iota((c, 1), 0)
    rown = _iota((n, 1), 0)
    ti, si = _iota((c, c), 0), _iota((c, c), 1)
    tn, sn = _iota((n, n), 0), _iota((n, n), 1)
    tri = ((sn <= tn) & (sn // c == tn // c)).astype(BF16)
    e64 = _block_ones(RW_W, RW_HD)
    rows = lambda fn: jnp.concatenate([fn(bi) for bi in range(nb)], axis=0)

    p = rows(lambda bi: prw_ref[bi])
    prev = rows(lambda bi: jnp.where(rowc == 0, carry_ref[bi], pltpu.roll(prw_ref[bi], shift=1, axis=0)))
    for bi in range(nb):
        carry_ref[bi] = prw_ref[bi, c - 1:c, :]
    rw = p + (prev - p) * mu_ref[...]
    r, k, v = rw[:, 0:256], rw[:, 256:512], rw[:, 512:768]
    tail = rw[:, 768:896]
    wd, ad, gd = tail[:, 0:32], tail[:, 32:64], tail[:, 64:128]
    w_log = -_softplus(-(w0_ref[...] + _mm(jnp.tanh(wd), w2_ref[...]))) - 0.5
    logw = -jnp.exp(w_log)
    a = _sigmoid(a0_ref[...] + _mm(ad, a2_ref[...]))
    g = _mm(_sigmoid(gd), g2_ref[...])
    if has_vres:
        vf = rows(lambda bi: vf_ref[bi])
        v = v + (vf - v) * _sigmoid(v0_ref[...] + _mm(_mm(v, v1_ref[...]), v2_ref[...]))
    else:
        for bi in range(nb):
            vout_ref[bi] = v[bi * c:(bi + 1) * c]
    kk = k * kk_ref[...]
    kk = kk * lax.rsqrt(_mm2(kk * kk, e64) + 1e-12)
    k2 = k * (1.0 + (a - 1.0) * ka_ref[...])
    bv = kk * a
    if t_valid < c:
        valid = rown % c < t_valid
        logw = jnp.where(valid, logw, 0.0)
        kk = jnp.where(valid, kk, 0.0)
        bv = jnp.where(valid, bv, 0.0)
        k2 = jnp.where(valid, k2, 0.0)
        v = jnp.where(valid, v, 0.0)
    g_incl = _mm2l(tri, logw)
    g_last = rows(lambda bi: jnp.broadcast_to(g_incl[(bi + 1) * c - 1:(bi + 1) * c], (c, RW_W)))
    e_neg = jnp.exp(-g_incl)
    e_last = jnp.exp(g_last - g_incl)
    kq = kk * jnp.exp(g_incl - logw)
    rq = r * jnp.exp(g_incl)
    bd, kd = bv * e_neg, k2 * e_neg
    bl, kl = bv * e_last, k2 * e_last
    s_decay = jnp.exp(g_last)

    chains = [(bi, h) for bi in range(nb) for h in range(RW_HEADS)]
    cut = lambda t: [t[bi * c:(bi + 1) * c, h * RW_HD:(h + 1) * RW_HD] for bi, h in chains]
    kq_c, rq_c, bd_c, kd_c, bl_c, kl_c, v_c = cut(kq), cut(rq), cut(bd), cut(kd), cut(bl), cut(kl), cut(v)
    s0_c = [s_ref[bi, h] for bi, h in chains]
    stack = lambda xs, ys: [jnp.concatenate([x, y], axis=0) for x, y in zip(xs, ys)]
    bk_c = stack(bd_c, kd_c)
    kbk = [_mm3(x, y, 1, 1) for x, y in zip(kq_c, bk_c)]
    rbk = [_mm(x, y, 1, 1) for x, y in zip(rq_c, bk_c)]
    a_b = [jnp.where(si < ti, x[:, 0:c], 0.0) for x in kbk]
    a_k = [jnp.where(si < ti, x[:, c:2 * c], 0.0) for x in kbk]
    m_b = [jnp.where(si <= ti, x[:, 0:c], 0.0) for x in rbk]
    m_k = [jnp.where(si <= ti, x[:, c:2 * c], 0.0) for x in rbk]
    krs0 = [_mm(x, s, 1, 1) for x, s in zip(stack(kq_c, rq_c), s0_c)]
    amv = [_mm(x, y) for x, y in zip(stack(a_k, m_k), v_c)]
    eye = (si == ti).astype(F32)
    t_inv = [eye - x for x in a_b]
    pw = a_b
    lvl = 2
    while lvl < c:
        pw = [_mm(x, x) for x in pw]
        t_inv = [x + _mm(x, y) for x, y in zip(t_inv, pw)]
        lvl *= 2
    u = [-_mm(x, y[0:c] + z[0:c]) for x, y, z in zip(t_inv, krs0, amv)]
    yh = [y[c:2 * c] + _mm(mb, uu) + z[c:2 * c] for y, mb, uu, z in zip(krs0, m_b, u, amv)]
    upd = [_mm(x, y, 0, 0) for x, y in zip(stack(u, v_c), stack(bl_c, kl_c))]
    for (bi, h), s0, dd in zip(chains, s0_c, upd):
        s_ref[bi, h] = s0 * s_decay[(bi + 1) * c - 1:(bi + 1) * c, h * RW_HD:(h + 1) * RW_HD] + dd

    y = rows(lambda bi: jnp.concatenate(yh[bi * RW_HEADS:(bi + 1) * RW_HEADS], axis=1))
    mean = _mm2(y, e64) * (1.0 / RW_HD)
    yc = y - mean
    var = _mm2(yc * yc, e64) * (1.0 / RW_HD)
    bonus = _mm2(r * k2 * rk_ref[...], e64) * v
    out = (yc * lax.rsqrt(var + RW_GN_EPS) * lng_ref[...] + lnb_ref[...] + bonus) * g
    for bi in range(nb):
        y_ref[bi] = out[bi * c:(bi + 1) * c]


def rwkv_mixer(prw, shift_proj, s0, v_first, p, t_valid):
    b, t, _ = prw.shape
    c = min(RW_CHUNK, t)
    nb = min(RW_BATCH * max(1, RW_CHUNK // (4 * c)), b)
    has_vres = v_first is not None
    tok = lambda w: pl.BlockSpec((nb, c, w), lambda i, j: (i, j, 0))
    per_b = lambda shape: pl.BlockSpec((nb,) + shape, lambda i, j: (i,) + (0,) * len(shape))
    vec = lambda a: a.reshape(1, -1)
    ins = [prw, shift_proj, s0]
    specs = [tok(RW_COLS), per_b((1, RW_COLS)), per_b((RW_HEADS, RW_HD, RW_HD))]
    if has_vres:
        ins.append(v_first)
        specs.append(tok(RW_W))
    params = [vec(p['rw_mu']), vec(p['rw_w0']), p['rw_w2'], vec(p['rw_a0']), p['rw_a2'], p['rw_g2']]
    if has_vres:
        params += [vec(p['rw_v0']), p['rw_v1'], p['rw_v2']]
    params += [vec(p['rw_k_k']), vec(p['rw_k_a']), vec(p['rw_r_k']), vec(p['rw_ln_g']), vec(p['rw_ln_b'])]
    ins += params
    specs += [_full(a.shape) for a in params]
    y_shape = jax.ShapeDtypeStruct((b, t, RW_W), F32)
    s_shape = jax.ShapeDtypeStruct((b, RW_HEADS, RW_HD, RW_HD), F32)
    s_spec = per_b((RW_HEADS, RW_HD, RW_HD))
    if has_vres:
        out_shape, out_specs = [y_shape, s_shape], [tok(RW_W), s_spec]
    else:
        out_shape, out_specs = [y_shape, y_shape, s_shape], [tok(RW_W), tok(RW_W), s_spec]
    outs = pl.pallas_call(
        functools.partial(_rwkv_kernel, has_vres, nb, c, t_valid),
        grid=(b // nb, t // c),
        in_specs=specs,
        out_specs=out_specs,
        out_shape=out_shape,
        scratch_shapes=[pltpu.VMEM((nb, 1, RW_COLS), F32)],
        compiler_params=_cparams(("parallel", "arbitrary")),
    )(*ins)
    if has_vres:
        return outs[0], v_first, outs[1]
    return outs[0], outs[1], outs[2]


def _gla_kernel(nb, c, sub, t_valid, pg_ref, s0_ref, a2_ref, ab_ref, ng_ref, y_ref, s_ref):
    @pl.when(pl.program_id(1) == 0)
    def _():
        s_ref[...] = s0_ref[...]

    expand_b = (_iota((GLA_KW, GLA_W), 0) // GLA_DK == _iota((GLA_KW, GLA_W), 1) // GLA_DV).astype(BF16)
    own_block = (_iota((GLA_W, GLA_KW), 0) // GLA_DV == _iota((GLA_W, GLA_KW), 1) // GLA_DK)
    e64 = _block_ones(GLA_W, GLA_DV)
    n = nb * c
    nsub = c // sub
    tn, sn = _iota((n, n), 0), _iota((n, n), 1)
    tri = ((sn <= tn) & (sn // sub == tn // sub)).astype(BF16)
    row = _iota((sub, 1), 0)
    rown = _iota((n, 1), 0)
    pg = jnp.concatenate([pg_ref[bi] for bi in range(nb)], axis=0)
    q = pg[:, 0:128] * (GLA_DK ** -0.5)
    k = pg[:, 128:256]
    v = pg[:, 256:512]
    rg = pg[:, 512:768]
    z = _mm(pg[:, 768:768 + GLA_A_LORA], a2_ref[...]) + ab_ref[...]
    la = (jnp.minimum(z, 0.0) - jnp.log(1.0 + jnp.exp(-jnp.abs(z)))) / GLA_TAU
    if t_valid < c:
        valid = rown % c < t_valid
        la = jnp.where(valid, la, 0.0)
        k = jnp.where(valid, k, 0.0)
    bcum = _mm2l(tri, la)
    qdec = q * jnp.exp(bcum)

    blocks = [(bi, j) for bi in range(nb) for j in range(nsub)]
    cut = lambda t: [t[bi * c + j * sub:bi * c + (j + 1) * sub] for bi, j in blocks]
    q_b, k_b, v_b, b_b, qd_b = cut(q), cut(k), cut(v), cut(bcum), cut(qdec)
    amat = []
    for qq, kk, bb in zip(q_b, k_b, b_b):
        prods = [jnp.where(row >= s, qq * kk[s:s + 1] * jnp.exp(jnp.minimum(bb - bb[s:s + 1], 0.0)), 0.0)
                 for s in range(sub)]
        amat.append(_mm(jnp.concatenate(prods, axis=0), expand_b))
    upd = [jnp.where(own_block, _mm(vv, kk * jnp.exp(bb[sub - 1:sub] - bb), 0, 0), 0.0)
           for vv, kk, bb in zip(v_b, k_b, b_b)]
    states = []
    for bi in range(nb):
        st = s_ref[bi]
        for j in range(nsub):
            states.append(st)
            i = bi * nsub + j
            st = st * jnp.exp(b_b[i][sub - 1:sub]) + upd[i]
        s_ref[bi] = st
    outs = []
    for i in range(len(blocks)):
        o = _mm(qd_b[i], states[i], 1, 1)
        for s in range(sub):
            o = o + amat[i][s * sub:(s + 1) * sub] * v_b[i][s:s + 1]
        outs.append(o)
    o = jnp.concatenate(outs, axis=0)
    ms = _mm2(o * o, e64) * (1.0 / GLA_DV)
    y = o * lax.rsqrt(ms + EPS) * ng_ref[...] * (rg * _sigmoid(rg))
    for bi in range(nb):
        y_ref[bi] = y[bi * c:(bi + 1) * c]


def gla_mixer(pg, s0t, p, t_valid):
    b, t, _ = pg.shape
    sub = min(GLA_SUB, t)
    c = min(4 * sub, t)
    nb = min(GLA_BATCH, b)
    tok = lambda w: pl.BlockSpec((nb, c, w), lambda i, j: (i, j, 0))
    st_spec = pl.BlockSpec((nb, GLA_W, GLA_KW), lambda i, j: (i, 0, 0))
    ng = jnp.tile(p['gla_norm_g'], GLA_HEADS).reshape(1, GLA_W)
    return pl.pallas_call(
        functools.partial(_gla_kernel, nb, c, sub, t_valid),
        grid=(b // nb, t // c),
        in_specs=[tok(GLA_PAD), st_spec, _full((GLA_A_LORA, GLA_KW)), _full((1, GLA_KW)), _full((1, GLA_W))],
        out_specs=[tok(GLA_W), st_spec],
        out_shape=[jax.ShapeDtypeStruct((b, t, GLA_W), F32), jax.ShapeDtypeStruct((b, GLA_W, GLA_KW), F32)],
        compiler_params=_cparams(("parallel", "arbitrary")),
    )(pg, s0t, p['gla_a2'], p['gla_ab'].reshape(1, GLA_KW), ng)


def _s5_kernel(rows, seg, carry_state, u_ref, hre_ref, him_ref, pre_ref, pim_ref, bre_ref, bim_ref,
               cre_ref, cim_ref, d_ref, ga_ref, gb_ref, y_ref, ore_ref, oim_ref, *scratch):
    u = u_ref[...]
    drive = _mm if carry_state else _mm3
    h_re = drive(u, bre_ref[...])
    h_im = drive(u, bim_ref[...])
    tpos = _iota((rows, 1), 0) % seg
    d = 1
    while d < seg:
        ar, ai = pre_ref[d - 1:d], pim_ref[d - 1:d]
        sr = jnp.where(tpos >= d, pltpu.roll(h_re, shift=d, axis=0), 0.0)
        sm = jnp.where(tpos >= d, pltpu.roll(h_im, shift=d, axis=0), 0.0)
        h_re, h_im = h_re + ar * sr - ai * sm, h_im + ar * sm + ai * sr
        d *= 2
    if carry_state:
        cre_s, cim_s = scratch

        @pl.when(pl.program_id(1) == 0)
        def _():
            cre_s[...] = hre_ref[0]
            cim_s[...] = him_ref[0]
        i_re, i_im = cre_s[...], cim_s[...]
        pr, pi = pre_ref[0:seg], pim_ref[0:seg]
        parts_re, parts_im = [], []
        for k in range(rows // seg):
            sl = slice(k * seg, (k + 1) * seg)
            p_re = h_re[sl] + pr * i_re - pi * i_im
            p_im = h_im[sl] + pr * i_im + pi * i_re
            i_re, i_im = p_re[seg - 1:seg], p_im[seg - 1:seg]
            parts_re.append(p_re)
            parts_im.append(p_im)
        h_re, h_im = jnp.concatenate(parts_re, axis=0), jnp.concatenate(parts_im, axis=0)
        cre_s[...] = i_re
        cim_s[...] = i_im
        ore_ref[0] = i_re
        oim_ref[0] = i_im
    else:
        i_re, i_im = hre_ref[...], him_ref[...]
        pr, pi = pre_ref[...], pim_ref[...]
        h_re, h_im = h_re + pr * i_re - pi * i_im, h_im + pr * i_im + pi * i_re
        ore_ref[...] = h_re
        oim_ref[...] = h_im
    y = _mm(h_re, cre_ref[...]) - _mm(h_im, cim_ref[...]) + d_ref[...] * u
    z = 0.5 * y * (1.0 + jnp.tanh(0.7978845608028654 * (y + 0.044715 * y * y * y)))
    y_ref[...] = _mm(z, ga_ref[...]) * _sigmoid(_mm(z, gb_ref[...]))


def s5_mixer(u, h0_re, h0_im, p, seq_len):
    n = u.shape[0]
    b = n // seq_len
    f32 = F32
    a_re, a_im = p['s5_a_re'], p['s5_a_im']
    dt = jnp.exp(p['s5_log_dt'])[:, None]
    mag = jnp.exp(a_re * dt)
    ab_re, ab_im = mag * jnp.cos(a_im * dt), mag * jnp.sin(a_im * dt)
    den = a_re * a_re + a_im * a_im
    cr = ((ab_re - 1.0) * a_re + ab_im * a_im) / den
    ci = (ab_im * a_re - (ab_re - 1.0) * a_im) / den
    bb_re = cr[..., None] * p['s5_b_re'] - ci[..., None] * p['s5_b_im']
    bb_im = cr[..., None] * p['s5_b_im'] + ci[..., None] * p['s5_b_re']
    eye_g = jnp.eye(S5_GROUPS, dtype=f32)

    def in_blockdiag(w):
        return jnp.einsum('gpi,gh->gihp', w, eye_g).reshape(S5_W, S5_STATE)

    def out_blockdiag(w):
        return jnp.einsum('gip,gh->gphi', w, eye_g).reshape(S5_STATE, S5_W)

    carry_state = seq_len > S5_TILE
    rows = S5_TILE
    seg = SUBLANE if carry_state else seq_len
    pw_re, pw_im = ab_re.reshape(1, S5_STATE), ab_im.reshape(1, S5_STATE)
    while pw_re.shape[0] < seg:
        tr, ti_ = pw_re[-1:], pw_im[-1:]
        pw_re, pw_im = (jnp.concatenate([pw_re, pw_re * tr - pw_im * ti_], axis=0),
                        jnp.concatenate([pw_im, pw_re * ti_ + pw_im * tr], axis=0))
    pw_re, pw_im = jnp.tile(pw_re, (rows // seg, 1)), jnp.tile(pw_im, (rows // seg, 1))
    weights = [pw_re, pw_im, in_blockdiag(bb_re), in_blockdiag(bb_im),
               out_blockdiag(p['s5_c_re']), out_blockdiag(p['s5_c_im']),
               p['s5_d'].reshape(1, S5_W), p['s5_glu_a'], p['s5_glu_b']]
    wspecs = [_full(w.shape) for w in weights]
    if carry_state:
        tiles = seq_len // rows
        grid = (b, tiles)
        tok = lambda w: pl.BlockSpec((rows, w), lambda i, j: (i * tiles + j, 0))
        st_in = pl.BlockSpec((1, 1, S5_STATE), lambda i, j: (i, 0, 0))
        ins = [u, h0_re.reshape(b, 1, S5_STATE), h0_im.reshape(b, 1, S5_STATE)]
        st_shape = jax.ShapeDtypeStruct((b, 1, S5_STATE), f32)
        scratch = [pltpu.VMEM((1, S5_STATE), f32)] * 2
        sem = ("parallel", "arbitrary")
        st_out = st_in
    else:
        grid = (n // rows,)
        tok = lambda w: pl.BlockSpec((rows, w), lambda i: (i, 0))
        st_in = tok(S5_STATE)
        ins = [u, jnp.repeat(h0_re, seq_len, axis=0), jnp.repeat(h0_im, seq_len, axis=0)]
        st_shape = jax.ShapeDtypeStruct((n, S5_STATE), f32)
        scratch = []
        sem = ("parallel",)
        st_out = st_in
    y, o_re, o_im = pl.pallas_call(
        functools.partial(_s5_kernel, rows, seg, carry_state),
        grid=grid,
        in_specs=[tok(S5_W), st_in, st_in] + wspecs,
        out_specs=[tok(S5_W), st_out, st_out],
        out_shape=[jax.ShapeDtypeStruct((n, S5_W), f32), st_shape, st_shape],
        scratch_shapes=scratch,
        compiler_params=_cparams(sem),
    )(*ins, *weights)
    if carry_state:
        return y, o_re.reshape(b, S5_STATE), o_im.reshape(b, S5_STATE)
    last = lambda h: h.reshape(b, seq_len, S5_STATE)[:, -1]
    return y, last(o_re), last(o_im)


def _rope_rotate(x, cos_t, sin_a, sin_b):
    w = x.shape[1]
    return x * cos_t + pltpu.roll(x, shift=w - MLA_ROPE // 2, axis=1) * sin_a + pltpu.roll(x, shift=MLA_ROPE // 2, axis=1) * sin_b


def _mla_keys(c, kr, cos_t, sin_a, sin_b, wuk_pad, place):
    k_nope = _mm(c, wuk_pad)
    k_rope = _rope_rotate(_mm3(kr, place), cos_t, sin_a, sin_b)
    return k_nope, k_rope


def _mla_prep_kernel(pm_ref, cos_ref, sina_ref, sinb_ref, qag_ref, wuq_ref, qg_ref, kvag_ref,
                     wuk_ref, wuv_ref, pk_ref, q_ref, k_ref, v_ref, c_ref, kr_ref):
    pm = pm_ref[...]
    qd, kvd, kr = pm[:, 0:256], pm[:, 256:512], pm[:, 512:512 + MLA_ROPE]
    tile4 = lambda t: jnp.concatenate([t] * MLA_HEADS, axis=1)
    cos_t, sin_a, sin_b = tile4(cos_ref[...]), tile4(sina_ref[...]), tile4(sinb_ref[...])
    e128 = _block_ones(MLA_HW, MLA_HP)
    qn = qd * lax.rsqrt(jnp.mean(qd * qd, axis=-1, keepdims=True) + EPS) * qag_ref[...]
    q = _mm(qn, wuq_ref[...])
    q = q * lax.rsqrt(_mm2(q * q, e128) * (1.0 / MLA_QK) + EPS) * qg_ref[...]
    q = _rope_rotate(q, cos_t, sin_a, sin_b)
    c = kvd * lax.rsqrt(jnp.mean(kvd * kvd, axis=-1, keepdims=True) + EPS) * kvag_ref[...]
    c_ref[...] = c
    kr_ref[...] = kr
    k_nope, k_rope = _mla_keys(c, kr, cos_t, sin_a, sin_b, wuk_ref[...], pk_ref[...])
    ssq = _mm2(k_nope * k_nope, e128) + jnp.sum(kr * kr, axis=-1, keepdims=True)
    kfull = (k_nope + k_rope) * (lax.rsqrt(ssq * (1.0 / MLA_QK) + EPS) * MLA_SCALE)
    vfull = _mm(c, wuv_ref[...])
    for h in range(MLA_HEADS):
        hs = slice(h * MLA_HP, (h + 1) * MLA_HP)
        q_ref[h] = _bf(q[:, hs])
        k_ref[h] = _bf(kfull[:, hs])
        v_ref[h] = _bf(vfull[:, hs])


def _flash_kernel(tq, tk, q_ref, k_ref, v_ref, o_ref):
    iq = pl.program_id(2)
    q = q_ref[0]
    ratio = tq // tk

    def step(ik, carry, masked):
        m, l, acc = carry
        off = pl.multiple_of(ik * tk, tk)
        s = _dg(q, k_ref[0, pl.ds(off, tk), :], 1, 1)
        if masked:
            s = jnp.where(off + _iota((tq, tk), 1) <= iq * tq + _iota((tq, tk), 0), s, NEG_INF)
        m_new = jnp.maximum(m, jnp.max(s, axis=-1, keepdims=True))
        alpha = jnp.exp(m - m_new)
        pexp = jnp.exp(s - m_new)
        l = alpha * l + jnp.sum(pexp, axis=-1, keepdims=True)
        acc = alpha * acc + jnp.dot(_bf(pexp), v_ref[0, pl.ds(off, tk), :], preferred_element_type=F32)
        return m_new, l, acc

    carry = (jnp.full((tq, 1), NEG_INF, F32), jnp.zeros((tq, 1), F32), jnp.zeros((tq, MLA_HP), F32))
    carry = lax.fori_loop(0, iq * ratio, lambda ik, cr: step(ik, cr, False), carry)
    for d in range(ratio):
        carry = step(iq * ratio + d, carry, True)
    m, l, acc = carry
    o_ref[...] = acc / l


def mla_prompt(pm, p, batch, seq_len, tables):
    n = pm.shape[0]
    rows = min(ROW_TILE, seq_len)
    tiles = seq_len // rows
    cos_t, sin_a, sin_b = tables
    tok = lambda w: pl.BlockSpec((rows, w), lambda i: (i, 0))
    tab = pl.BlockSpec((rows, MLA_HP), lambda i: (i % tiles, 0))
    hd = pl.BlockSpec((MLA_HEADS, rows, MLA_HP), lambda i: (0, i, 0))
    weights = [p['qag'], p['wuq_pad'], p['qg_pad'], p['kvag'], p['wuk_pad'], p['wuv_pad'], p['pk']]
    hshape = jax.ShapeDtypeStruct((MLA_HEADS, n, MLA_HP), BF16)
    q, k, v, c, kr = pl.pallas_call(
        _mla_prep_kernel,
        grid=(n // rows,),
        in_specs=[tok(MLA_PAD), tab, tab, tab] + [_full(w.shape) for w in weights],
        out_specs=[hd, hd, hd, tok(MLA_LORA), tok(MLA_ROPE)],
        out_shape=[hshape, hshape, hshape, jax.ShapeDtypeStruct((n, MLA_LORA), F32),
                   jax.ShapeDtypeStruct((n, MLA_ROPE), F32)],
        compiler_params=_cparams(("parallel",)),
    )(pm, cos_t, sin_a, sin_b, *weights)
    tq = min(ATT_TQ, seq_len)
    tk = min(ATT_TK, tq)
    nq = seq_len // tq
    kv_spec = pl.BlockSpec((1, seq_len, MLA_HP), lambda b, h, i: (h, b, 0))
    o = pl.pallas_call(
        functools.partial(_flash_kernel, tq, tk),
        grid=(batch, MLA_HEADS, nq),
        in_specs=[pl.BlockSpec((1, tq, MLA_HP), lambda b, h, i: (h, b * nq + i, 0)), kv_spec, kv_spec],
        out_specs=pl.BlockSpec((tq, MLA_HP), lambda b, h, i: (b * nq + i, h)),
        out_shape=jax.ShapeDtypeStruct((n, MLA_HW), F32),
        compiler_params=_cparams(("parallel", "parallel", "arbitrary")),
    )(q, k, v)
    return o, c, kr


def _mla_decode_kernel(n_pages, t_new, *refs):
    pm_ref, tabn_ref, tabp_ref, vec_ref, w3_ref, pkg_ref, wuk_ref = refs[1:8]
    c_refs = refs[8:8 + n_pages]
    kr_refs = refs[8 + n_pages:8 + 2 * n_pages]
    o_ref, c_out, kr_out, q_s, qn_s, qr_s, m_s, l_s, acc_s, cbf_s = refs[8 + 2 * n_pages:]
    qag, kvag, qg = vec_ref[0:1, 0:MLA_LORA], vec_ref[1:2, 0:MLA_LORA], vec_ref[2:3, :]
    place, kgain = pkg_ref[:, 0:MLA_HW], pkg_ref[:, MLA_HW:MLA_HW + 1]
    j = pl.program_id(1)
    nq = MLA_HEADS * SUBLANE
    n = n_pages * PAGE_SIZE
    half = MLA_ROPE // 2
    tile4 = lambda t: jnp.concatenate([t] * MLA_HEADS, axis=1)
    row_head = (_iota((nq, MLA_HW), 0) // SUBLANE == _iota((nq, MLA_HW), 1) // MLA_HP)
    head_sel = (_iota((nq, MLA_LORA), 0) // SUBLANE == _iota((nq, MLA_LORA), 1) // MLA_NOPE)

    def softmax_step(s, values):
        m_old = m_s[...]
        m_new = jnp.maximum(m_old, jnp.max(s, axis=-1, keepdims=True))
        alpha = jnp.exp(m_old - m_new)
        pexp = jnp.exp(s - m_new)
        l_s[...] = alpha * l_s[...] + jnp.sum(pexp, axis=-1, keepdims=True)
        m_s[...] = m_new
        acc_s[...] = acc_s[...] * alpha + _mm(pexp, values)

    pm = pm_ref[0]
    qd, kvd, kr_new = pm[:, 0:256], pm[:, 256:512], pm[:, 512:512 + MLA_ROPE]
    c_new = kvd * lax.rsqrt(jnp.mean(kvd * kvd, axis=-1, keepdims=True) + EPS) * kvag
    cos_n, sin_an, sin_bn = (tile4(tabn_ref[:, 0:MLA_HP]), tile4(tabn_ref[:, MLA_HP:2 * MLA_HP]),
                             tile4(tabn_ref[:, 2 * MLA_HP:3 * MLA_HP]))

    @pl.when(j == 0)
    def _():
        e128 = _block_ones(MLA_HW, MLA_HP)
        qn = qd * lax.rsqrt(jnp.mean(qd * qd, axis=-1, keepdims=True) + EPS) * qag
        q = _mm(qn, w3_ref[:, 0:MLA_HW])
        q = q * lax.rsqrt(_mm2(q * q, e128) * (1.0 / MLA_QK) + EPS) * qg
        q = _rope_rotate(q, cos_n, sin_an, sin_bn)
        q4 = jnp.concatenate([q] * MLA_HEADS, axis=0)
        q_s[...] = jnp.where(row_head, q4, 0.0)
        li, lo = _iota((MLA_HW, MLA_LORA), 0), _iota((MLA_HW, MLA_LORA), 1)
        take_nope = ((li // MLA_HP == lo // MLA_NOPE) & (li % MLA_HP == lo % MLA_NOPE)).astype(BF16)
        qn_s[...] = _bf(jnp.where(head_sel, _mm(q4, take_nope), 0.0))
        ri, ro = _iota((MLA_HW, MLA_ROPE), 0), _iota((MLA_HW, MLA_ROPE), 1)
        take_rope = (ri % MLA_HP == MLA_NOPE + ro).astype(BF16)
        qr_s[...] = _bf(_mm(jnp.where(row_head, q4, 0.0), take_rope))
        m_s[...] = jnp.full(m_s.shape, NEG_INF, F32)
        l_s[...] = jnp.zeros(l_s.shape, F32)
        acc_s[...] = jnp.zeros(acc_s.shape, F32)
        c_out[0] = c_new
        kr_out[0] = kr_new

    for g in range(n_pages):
        cbf_s[g * PAGE_SIZE:(g + 1) * PAGE_SIZE, :] = _bf(c_refs[g][0, 0])
    kt = jnp.concatenate([kr_refs[g][0, 0] for g in range(n_pages)], axis=1)
    sq_rope = jnp.sum(kt * kt, axis=0, keepdims=True)
    kg = kt * kgain
    x1, x2 = kg[0:half], kg[half:]
    cos_p, sin_p = tabp_ref[0:half, :], tabp_ref[half:MLA_ROPE, :]
    krot = jnp.concatenate([x1 * cos_p - x2 * sin_p, x2 * cos_p + x1 * sin_p], axis=0)
    s_rope = jnp.dot(qr_s[...], _bf(krot), preferred_element_type=F32)
    nt = n // DEC_SUB
    sel_b = _bf(head_sel.astype(F32))
    lanes = lambda t, i: t[:, i * DEC_SUB:(i + 1) * DEC_SUB]
    val = [dict() for _ in range(nt)]

    def stage(k, i):
        d = val[i]
        if k == 0:
            d['cb'] = cbf_s[i * DEC_SUB:(i + 1) * DEC_SUB, :]
            d['kpre'] = jnp.dot(d['cb'], wuk_ref[...], preferred_element_type=F32)
        elif k == 1:
            d['ksq'], d['kb'] = _bf(d['kpre'] * d['kpre']), _bf(d['kpre'])
        elif k == 2:
            d['ssq'] = _dg(sel_b, d['ksq'], 1, 1)
            d['sc'] = _dg(qn_s[...], d['kb'], 1, 1)
        elif k == 3:
            norm = lax.rsqrt((d['ssq'] + lanes(sq_rope, i)) * (1.0 / MLA_QK) + EPS) * MLA_SCALE
            d['s'] = (d['sc'] + lanes(s_rope, i)) * norm
            d['m'] = jnp.max(d['s'], axis=-1, keepdims=True)
        elif k == 4:
            pexp = jnp.exp(d['s'] - d['m'])
            d['l'] = jnp.sum(pexp, axis=-1, keepdims=True)
            d['acc'] = jnp.dot(_bf(pexp), d['cb'], preferred_element_type=F32)

    n_stage = 5
    for t in range(nt + n_stage - 1):
        for k in range(n_stage):
            if 0 <= t - k < nt:
                stage(k, t - k)
    m_old = m_s[...]
    m_new = m_old
    for d in val:
        m_new = jnp.maximum(m_new, d['m'])
    alpha = jnp.exp(m_old - m_new)
    l_new = alpha * l_s[...]
    acc = acc_s[...] * alpha
    for d in val:
        w = jnp.exp(d['m'] - m_new)
        l_new = l_new + w * d['l']
        acc = acc + w * d['acc']
    m_s[...] = m_new
    l_s[...] = l_new
    acc_s[...] = acc

    @pl.when(j == pl.num_programs(1) - 1)
    def _():
        k_nope, k_rope = _mla_keys(c_new, kr_new, cos_n, sin_an, sin_bn, w3_ref[:, MLA_HW:2 * MLA_HW], place)
        ssq_n = _mm2l(row_head.astype(BF16), k_nope * k_nope, 1, 1)
        ssq_n = ssq_n + _mm2l(jnp.ones((SUBLANE, MLA_ROPE), BF16), kr_new * kr_new, 1, 1)[0:1]
        s_n = _mm(q_s[...], k_nope + k_rope, 1, 1) * (lax.rsqrt(ssq_n * (1.0 / MLA_QK) + EPS) * MLA_SCALE)
        qry_t = _iota((nq, SUBLANE), 0) % SUBLANE
        key_t = _iota((nq, SUBLANE), 1)
        softmax_step(jnp.where((key_t <= qry_t) & (key_t < t_new), s_n, NEG_INF), c_new)
        o_lat = acc_s[...] / l_s[...]
        full = jnp.where(row_head, _mm(o_lat, w3_ref[:, 2 * MLA_HW:3 * MLA_HW]), 0.0)
        gather = (_iota((SUBLANE, nq), 0) == _iota((SUBLANE, nq), 1) % SUBLANE).astype(BF16)
        o_ref[0] = _mm2l(gather, full)


def mla_decode(pm, cache_c, cache_krt, page_table, layer, p, tables_new, tables_past_t, t_new):
    b = pm.shape[0]
    pages = page_table.shape[1]
    steps = pages // DEC_PAGES
    n = DEC_PAGES * PAGE_SIZE
    per_b = lambda w: pl.BlockSpec((1, SUBLANE, w), lambda i, j, pt: (i, 0, 0))
    tab_new = pl.BlockSpec((SUBLANE, 3 * MLA_HP), lambda i, j, pt: (0, 0))
    tab_past = pl.BlockSpec((MLA_ROPE, n), lambda i, j, pt: (0, j))
    pad_row = lambda v: jnp.pad(v, ((0, 0), (0, MLA_HW - v.shape[1])))
    vecs = jnp.pad(jnp.concatenate([pad_row(p['qag']), pad_row(p['kvag']), p['qg_pad']], axis=0),
                   ((0, SUBLANE - 3), (0, 0)))
    w3 = jnp.concatenate([p['wuq_pad'], p['wuk_pad'], p['wuv_pad']], axis=1)
    pkg = jnp.concatenate([p['pk'], p['kgain_col'], jnp.zeros((MLA_ROPE, LANE - 1), F32)], axis=1)
    weights = [vecs, w3, pkg, p['wuk']]
    wspecs = [pl.BlockSpec(w.shape, lambda i, j, pt, nd=w.ndim: (0,) * nd) for w in weights]

    def page_spec(rows, w, slot):
        return pl.BlockSpec((1, 1, rows, w), lambda i, j, pt: (layer, pt[i, j * DEC_PAGES + slot], 0, 0))

    c_specs = [page_spec(PAGE_SIZE, MLA_LORA, s) for s in range(DEC_PAGES)]
    kr_specs = [page_spec(MLA_ROPE, PAGE_SIZE, s) for s in range(DEC_PAGES)]
    nq = MLA_HEADS * SUBLANE
    grid_spec = pltpu.PrefetchScalarGridSpec(
        num_scalar_prefetch=1,
        grid=(b, steps),
        in_specs=[per_b(MLA_PAD), tab_new, tab_past] + wspecs + c_specs + kr_specs,
        out_specs=[per_b(MLA_HW), per_b(MLA_LORA), per_b(MLA_ROPE)],
        scratch_shapes=[pltpu.VMEM((nq, MLA_HW), F32), pltpu.VMEM((nq, MLA_LORA), BF16),
                        pltpu.VMEM((nq, MLA_ROPE), BF16), pltpu.VMEM((nq, 1), F32),
                        pltpu.VMEM((nq, 1), F32), pltpu.VMEM((nq, MLA_LORA), F32),
                        pltpu.VMEM((n, MLA_LORA), BF16)],
    )
    o, c, kr = pl.pallas_call(
        functools.partial(_mla_decode_kernel, DEC_PAGES, t_new),
        grid_spec=grid_spec,
        out_shape=[jax.ShapeDtypeStruct((b, SUBLANE, MLA_HW), F32),
                   jax.ShapeDtypeStruct((b, SUBLANE, MLA_LORA), F32),
                   jax.ShapeDtypeStruct((b, SUBLANE, MLA_ROPE), F32)],
        compiler_params=_cparams(("parallel", "arbitrary")),
    )(page_table, pm, jnp.concatenate(tables_new, axis=1), jnp.concatenate(tables_past_t, axis=0), *weights,
      *([cache_c] * DEC_PAGES), *([cache_krt] * DEC_PAGES))
    return o, c, kr


def rope_tables_transposed(pos):
    half = MLA_ROPE // 2
    inv = ROPE_THETA ** (-jnp.arange(half, dtype=F32) / half)
    ang = inv[:, None] * pos.astype(F32)[None, :]
    return jnp.cos(ang), jnp.sin(ang)


def rope_tables(pos):
    half = MLA_ROPE // 2
    inv = ROPE_THETA ** (-jnp.arange(half, dtype=F32) / half)
    ang = pos.astype(F32)[:, None] * inv
    cos, sin = jnp.cos(ang), jnp.sin(ang)
    n = pos.shape[0]
    ones, zeros = jnp.ones((n, MLA_NOPE), F32), jnp.zeros((n, MLA_NOPE), F32)
    zpad = jnp.zeros((n, MLA_HP - MLA_QK), F32)
    zh = jnp.zeros((n, half), F32)
    cos_t = jnp.concatenate([ones, cos, cos, zpad + 1.0], axis=1)
    sin_a = jnp.concatenate([zeros, -sin, zh, zpad], axis=1)
    sin_b = jnp.concatenate([zeros, zh, sin, zpad], axis=1)
    return cos_t, sin_a, sin_b


def _merge_kernel(x_ref, xn_ref, yrw_ref, ygla_ref, ys5_ref, ymla_ref, wg_ref, wb_ref, wbm_ref, wo_ref, o_ref):
    xb = _bf(xn_ref[...])
    merged = None
    for n, y_ref in enumerate((yrw_ref, ygla_ref, ys5_ref, ymla_ref)):
        gate = _sigmoid(jnp.dot(xb, wg_ref[:, n * D_MODEL:(n + 1) * D_MODEL], preferred_element_type=F32))
        w = wbm_ref[...] if n == 3 else wb_ref[n]
        term = gate * jnp.dot(_bf(y_ref[...]), w, preferred_element_type=F32)
        merged = term if merged is None else merged + term
    o_ref[...] = x_ref[...] + jnp.dot(_bf(merged), wo_ref[...], preferred_element_type=F32)


def merge(x, xn, y_rw, y_gla, y_s5, y_mla, p):
    n = x.shape[0]
    row = lambda w: pl.BlockSpec((ROW_TILE, w), lambda i: (i, 0))
    weights = [p['w_gate'], p['w_branch3'], p['w_branch_mla'], p['w_out']]
    return pl.pallas_call(
        _merge_kernel,
        grid=(n // ROW_TILE,),
        in_specs=[row(D_MODEL), row(D_MODEL), row(BRANCH_W), row(BRANCH_W), row(BRANCH_W), row(MLA_HW)]
        + [_full(w.shape) for w in weights],
        out_specs=row(D_MODEL),
        out_shape=jax.ShapeDtypeStruct((n, D_MODEL), F32),
        compiler_params=_cparams(("parallel",)),
    )(x, xn, y_rw, y_gla, y_s5, y_mla, *weights)


def _mlp_kernel(ff_chunk, x_ref, g_ref, w1_ref, w2_ref, o_ref):
    x = x_ref[...]
    xb = _bf(x * lax.rsqrt(jnp.mean(x * x, axis=-1, keepdims=True) + EPS) * g_ref[...])
    acc = x
    for o in range(0, D_FF, ff_chunk):
        h = jnp.maximum(jnp.dot(xb, w1_ref[:, o:o + ff_chunk], preferred_element_type=F32), 0.0)
        acc = acc + jnp.dot(_bf(h * h), w2_ref[o:o + ff_chunk, :], preferred_element_type=F32)
    o_ref[...] = acc


def mlp(x, p):
    n = x.shape[0]
    row = pl.BlockSpec((ROW_TILE, D_MODEL), lambda i: (i, 0))
    return pl.pallas_call(
        functools.partial(_mlp_kernel, 1024),
        grid=(n // ROW_TILE,),
        in_specs=[row, _full((1, D_MODEL)), _full((D_MODEL, D_FF)), _full((D_FF, D_MODEL))],
        out_specs=row,
        out_shape=jax.ShapeDtypeStruct((n, D_MODEL), F32),
        compiler_params=_cparams(("parallel",)),
    )(x, p['norm2_g'].reshape(1, D_MODEL), p['w_ff1'], p['w_ff2'])


def _pad_cols(w, width):
    return jnp.pad(w, ((0, 0), (0, width - w.shape[1])))


def _head_pad(w, used):
    kdim = w.shape[0]
    w = w.reshape(kdim, MLA_HEADS, used)
    return jnp.pad(w, ((0, 0), (0, 0), (0, MLA_HP - used))).reshape(kdim, MLA_HW)


def prepare_layer(l, W):
    p = {}
    w_in = W['w_in'][l]
    o_gla, o_s5, o_mla = RW_COLS, RW_COLS + GLA_COLS, RW_COLS + GLA_COLS + S5_W
    w_gla = w_in[:, o_gla:o_s5]
    w_gla = jnp.concatenate([w_gla[:, 0:512], w_gla[:, 528:784], w_gla[:, 512:528]], axis=1)
    p['w_rw'] = _bf(w_in[:, :RW_COLS])
    p['w_all'] = _bf(jnp.concatenate([w_in[:, :RW_COLS], _pad_cols(w_gla, GLA_PAD), w_in[:, o_s5:o_mla],
                                      _pad_cols(w_in[:, o_mla:], MLA_PAD)], axis=1))
    for name in ('norm1_g', 'rw_mu', 'rw_w0', 'rw_w2', 'rw_a0', 'rw_a2', 'rw_g2', 'rw_k_k', 'rw_k_a',
                 'rw_ln_g', 'rw_ln_b', 'gla_a2', 'gla_ab', 'gla_norm_g', 's5_a_re', 's5_a_im', 's5_log_dt',
                 's5_b_re', 's5_b_im', 's5_c_re', 's5_c_im', 's5_glu_a', 's5_glu_b', 'norm2_g'):
        p[name] = W[name][l]
    p['rw_r_k'] = W['rw_r_k'][l].reshape(RW_W)
    p['s5_d'] = W['s5_d'][l].reshape(S5_W)
    if l > 0:
        for name in ('rw_v0', 'rw_v1', 'rw_v2'):
            p[name] = W[name][l - 1]
    k_g = W['mla_k_g'][l]
    p['qag'] = W['mla_qa_g'][l].reshape(1, MLA_LORA)
    p['kvag'] = W['mla_kva_g'][l].reshape(1, MLA_LORA)
    p['wuq_pad'] = _head_pad(W['mla_w_uq'][l], MLA_QK)
    qg = W['mla_q_g'][l] * jnp.concatenate([k_g[:MLA_NOPE], jnp.ones((MLA_ROPE,), F32)])
    p['qg_pad'] = jnp.tile(jnp.pad(qg, (0, MLA_HP - MLA_QK)), MLA_HEADS).reshape(1, MLA_HW)
    p['wuk_pad'] = _head_pad(W['mla_w_uk'][l].reshape(MLA_LORA, MLA_HEADS * MLA_NOPE), MLA_NOPE)
    wuv = W['mla_w_uv'][l].reshape(MLA_LORA, MLA_HEADS * MLA_V)
    p['wuv_pad'] = _head_pad(wuv, MLA_V)
    p['wuk'] = _bf(W['mla_w_uk'][l].reshape(MLA_LORA, MLA_HEADS * MLA_NOPE))
    p['kgain_col'] = k_g[MLA_NOPE:].reshape(MLA_ROPE, 1)
    place = jnp.pad(jnp.diag(k_g[MLA_NOPE:]), ((0, 0), (MLA_NOPE, MLA_HP - MLA_QK)))
    p['pk'] = jnp.tile(place, (1, MLA_HEADS))
    p['w_gate'] = _bf(W['w_gate'][l])
    wb = W['w_branch'][l]
    p['w_branch3'] = _bf(wb)
    wbm = wb[3].reshape(MLA_HEADS, MLA_V, D_MODEL)
    p['w_branch_mla'] = _bf(jnp.pad(wbm, ((0, 0), (0, MLA_HP - MLA_V), (0, 0))).reshape(MLA_HW, D_MODEL))
    p['w_out'] = _bf(W['w_out'][l])
    p['w_ff1'] = _bf(W['w_ff1'][l])
    p['w_ff2'] = _bf(W['w_ff2'][l])
    return p


def _gla_state_in(s):
    eye = jnp.eye(GLA_HEADS, dtype=F32)
    return jnp.einsum('bhdv,hg->bhvgd', s, eye).reshape(s.shape[0], GLA_W, GLA_KW)


def _gla_state_out(st):
    b = st.shape[0]
    st = st.reshape(b, GLA_HEADS, GLA_DV, GLA_HEADS, GLA_DK)
    return jnp.stack([st[:, h, :, h, :] for h in range(GLA_HEADS)], axis=1).swapaxes(2, 3)


def kernel(x_prompt, x_sample, cache_mla_latent, cache_mla_krope, page_table, state_rwkv, state_shift, state_gla, state_s5_re, state_s5_im, norm1_g, w_in, rw_mu, rw_w0, rw_w2, rw_a0, rw_a2, rw_g2, rw_v0, rw_v1, rw_v2, rw_k_k, rw_k_a, rw_r_k, rw_ln_g, rw_ln_b, gla_a2, gla_ab, gla_norm_g, s5_a_re, s5_a_im, s5_log_dt, s5_b_re, s5_b_im, s5_c_re, s5_c_im, s5_d, s5_glu_a, s5_glu_b, mla_qa_g, mla_w_uq, mla_kva_g, mla_w_uk, mla_w_uv, mla_q_g, mla_k_g, w_gate, w_branch, w_out, norm2_g, w_ff1, w_ff2):
    W = dict(norm1_g=norm1_g, w_in=w_in, rw_mu=rw_mu, rw_w0=rw_w0, rw_w2=rw_w2, rw_a0=rw_a0, rw_a2=rw_a2,
             rw_g2=rw_g2, rw_v0=rw_v0, rw_v1=rw_v1, rw_v2=rw_v2, rw_k_k=rw_k_k, rw_k_a=rw_k_a, rw_r_k=rw_r_k,
             rw_ln_g=rw_ln_g, rw_ln_b=rw_ln_b, gla_a2=gla_a2, gla_ab=gla_ab, gla_norm_g=gla_norm_g,
             s5_a_re=s5_a_re, s5_a_im=s5_a_im, s5_log_dt=s5_log_dt, s5_b_re=s5_b_re, s5_b_im=s5_b_im,
             s5_c_re=s5_c_re, s5_c_im=s5_c_im, s5_d=s5_d, s5_glu_a=s5_glu_a, s5_glu_b=s5_glu_b,
             mla_qa_g=mla_qa_g, mla_w_uq=mla_w_uq, mla_kva_g=mla_kva_g, mla_w_uk=mla_w_uk, mla_w_uv=mla_w_uv,
             mla_q_g=mla_q_g, mla_k_g=mla_k_g, w_gate=w_gate, w_branch=w_branch, w_out=w_out,
             norm2_g=norm2_g, w_ff1=w_ff1, w_ff2=w_ff2)
    bp, tp, _ = x_prompt.shape
    bs, ts, _ = x_sample.shape
    depth = w_in.shape[0]
    np_, ns = bp * tp, bs * ts
    past_len = page_table.shape[1] * PAGE_SIZE
    tpad = SUBLANE

    tables_p = rope_tables(jnp.arange(tp, dtype=jnp.int32))
    tables_new = rope_tables(past_len + jnp.arange(tpad, dtype=jnp.int32))
    tables_past = rope_tables_transposed(jnp.arange(past_len, dtype=jnp.int32))
    cache_krt = jnp.swapaxes(cache_mla_krope, 2, 3)

    xp, xs = x_prompt.reshape(np_, D_MODEL), x_sample.reshape(ns, D_MODEL)
    pad_t = lambda a: jnp.pad(a.reshape(bs, ts, a.shape[-1]), ((0, 0), (0, tpad - ts), (0, 0)))
    unpad = lambda a: a[:, :ts].reshape(ns, a.shape[-1])
    vf_p = vf_s = None
    outs = {k: [] for k in ('c_p', 'kr_p', 'c_s', 'kr_s', 'rw_p', 'rw_s', 'sh_p', 'sh_s', 'gla_p', 'gla_s',
                            're_p', 'im_p', 're_s', 'im_s')}
    zeros_p = lambda *shape: jnp.zeros((bp,) + shape, F32)
    for l in range(depth):
        p = prepare_layer(l, W)
        xn_p, prw_p, pgla_p, ps5_p, pmla_p = norm_proj(xp, p['norm1_g'], p['w_all'])
        xn_s, prw_s, pgla_s, ps5_s, pmla_s = norm_proj(xs, p['norm1_g'], p['w_all'])
        outs['sh_p'].append(xn_p.reshape(bp, tp, D_MODEL)[:, -1])
        outs['sh_s'].append(xn_s.reshape(bs, ts, D_MODEL)[:, -1])

        y_rw_p, vf_p, s_rw_p = rwkv_mixer(prw_p.reshape(bp, tp, RW_COLS), zeros_p(1, RW_COLS),
                                          zeros_p(RW_HEADS, RW_HD, RW_HD), vf_p, p, tp)
        shift_proj = small_matmul(state_shift[l], p['w_rw']).reshape(bs, 1, RW_COLS)
        y_rw_s, vf_s, s_rw_s = rwkv_mixer(pad_t(prw_s), shift_proj, state_rwkv[l], vf_s, p, ts)
        outs['rw_p'].append(s_rw_p)
        outs['rw_s'].append(s_rw_s)

        y_gla_p, s_gla_p = gla_mixer(pgla_p.reshape(bp, tp, GLA_PAD), zeros_p(GLA_W, GLA_KW), p, tp)
        y_gla_s, s_gla_s = gla_mixer(pad_t(pgla_s), _gla_state_in(state_gla[l]), p, ts)
        outs['gla_p'].append(_gla_state_out(s_gla_p))
        outs['gla_s'].append(_gla_state_out(s_gla_s))

        y_s5_p, re_p, im_p = s5_mixer(ps5_p, zeros_p(S5_STATE), zeros_p(S5_STATE), p, tp)
        y_s5_s, re_s, im_s = s5_mixer(ps5_s, state_s5_re[l].reshape(bs, S5_STATE),
                                      state_s5_im[l].reshape(bs, S5_STATE), p, ts)
        for key, val, bb in (('re_p', re_p, bp), ('im_p', im_p, bp), ('re_s', re_s, bs), ('im_s', im_s, bs)):
            outs[key].append(val.reshape(bb, S5_GROUPS, S5_P))

        y_mla_p, c_p, kr_p = mla_prompt(pmla_p, p, bp, tp, tables_p)
        y_mla_s, c_s, kr_s = mla_decode(pad_t(pmla_s), cache_mla_latent, cache_krt, page_table, l, p,
                                        tables_new, tables_past, ts)
        outs['c_p'].append(c_p.reshape(bp, tp, MLA_LORA))
        outs['kr_p'].append(kr_p.reshape(bp, tp, MLA_ROPE))
        outs['c_s'].append(c_s[:, :ts])
        outs['kr_s'].append(kr_s[:, :ts])

        xp = mlp(merge(xp, xn_p, y_rw_p.reshape(np_, RW_W), y_gla_p.reshape(np_, GLA_W), y_s5_p, y_mla_p, p), p)
        xs = mlp(merge(xs, xn_s, unpad(y_rw_s), unpad(y_gla_s), y_s5_s, unpad(y_mla_s), p), p)

    st = lambda k: jnp.stack(outs[k], axis=0)
    return (xp.reshape(bp, tp, D_MODEL), xs.reshape(bs, ts, D_MODEL),
            st('c_p'), st('kr_p'), st('c_s'), st('kr_s'),
            st('rw_p'), st('rw_s'), st('sh_p'), st('sh_s'),
            st('gla_p'), st('gla_s'), st('re_p'), st('im_p'), st('re_s'), st('im_s'))
```

```python
import functools

import jax
import jax.numpy as jnp
from jax import lax
from jax.experimental import pallas as pl
from jax.experimental.pallas import tpu as pltpu

F32 = jnp.float32
BF16 = jnp.bfloat16

D_MODEL = 1024
N_BRANCH = 4
BRANCH_W = 256
EPS = 1e-6
PAGE_SIZE = 128
RW_HEADS, RW_HD, RW_W = 4, 64, 256
RW_COLS = 896
RW_GN_EPS = 64e-5
GLA_HEADS, GLA_DK, GLA_DV, GLA_W = 4, 32, 64, 256
GLA_KW = GLA_HEADS * GLA_DK
GLA_A_LORA = 16
GLA_TAU = 16.0
GLA_COLS = 784
GLA_PAD = 896
S5_GROUP, S5_GROUPS, S5_P, S5_W = 16, 16, 64, 256
S5_STATE = S5_GROUPS * S5_P
MLA_HEADS, MLA_NOPE, MLA_ROPE, MLA_QK, MLA_V = 4, 64, 32, 96, 64
MLA_LORA = 256
MLA_COLS = 544
MLA_PAD = 640
MLA_HP = 128
MLA_HW = MLA_HEADS * MLA_HP
MLA_SCALE = MLA_QK ** -0.5
ROPE_THETA = 10000.0
NEG_INF = -1e30
D_FF = 4096
IN_PAD = RW_COLS + GLA_PAD + S5_W + MLA_PAD

V7X_VMEM_BYTES = 64 * 1024 * 1024
VMEM_LIMIT = V7X_VMEM_BYTES - 8 * 1024 * 1024
LANE = 128
SUBLANE = 8

ROW_TILE = 512
RW_CHUNK = 64
RW_BATCH = 4
GLA_SUB = 16
GLA_BATCH = 4
S5_TILE = 128
ATT_TQ = 1024
ATT_TK = 512
DEC_PAGES = 64
DEC_SUB = 512


def _cparams(sem):
    return pltpu.CompilerParams(dimension_semantics=sem, vmem_limit_bytes=VMEM_LIMIT)


def _bf(x):
    return x.astype(BF16)


def _dg(a, b, ca, cb):
    return lax.dot_general(a, b, (((ca,), (cb,)), ((), ())), preferred_element_type=F32)


def _mm(a, b, ca=1, cb=0):
    return _dg(_bf(a), _bf(b), ca, cb)


def _split(x):
    hi = _bf(x)
    return hi, _bf(x - hi.astype(F32))


def _mm3(a, b, ca=1, cb=0):
    ah, al = _split(a)
    bh, bl = _split(b)
    return _dg(ah, bh, ca, cb) + _dg(al, bh, ca, cb) + _dg(ah, bl, ca, cb)


def _mm2(a, b01, ca=1, cb=0):
    ah, al = _split(a)
    return _dg(ah, b01, ca, cb) + _dg(al, b01, ca, cb)


def _mm2l(a01, b, ca=1, cb=0):
    bh, bl = _split(b)
    return _dg(a01, bh, ca, cb) + _dg(a01, bl, ca, cb)


def _sigmoid(x):
    return 1.0 / (1.0 + jnp.exp(-x))


def _softplus(x):
    return jnp.maximum(x, 0.0) + jnp.log(1.0 + jnp.exp(-jnp.abs(x)))


def _iota(shape, dim):
    return lax.broadcasted_iota(jnp.int32, shape, dim)


def _block_ones(n, blk, dtype=BF16):
    return (_iota((n, n), 0) // blk == _iota((n, n), 1) // blk).astype(dtype)


def _full(shape):
    nd = len(shape)
    return pl.BlockSpec(shape, lambda *_: (0,) * nd)


def _norm_proj_kernel(x_ref, g_ref, w_ref, xn_ref, rw_ref, gla_ref, s5_ref, mla_ref):
    x = x_ref[...]
    xn = x * lax.rsqrt(jnp.mean(x * x, axis=-1, keepdims=True) + EPS) * g_ref[...]
    xn_ref[...] = xn
    xb = _bf(xn)
    o = 0
    for ref in (rw_ref, gla_ref, s5_ref, mla_ref):
        w = ref.shape[1]
        ref[...] = jnp.dot(xb, w_ref[:, o:o + w], preferred_element_type=F32)
        o += w


def norm_proj(x, g, w_all):
    n = x.shape[0]
    widths = (RW_COLS, GLA_PAD, S5_W, MLA_PAD)
    row = lambda w: pl.BlockSpec((ROW_TILE, w), lambda i: (i, 0))
    return pl.pallas_call(
        _norm_proj_kernel,
        grid=(n // ROW_TILE,),
        in_specs=[row(D_MODEL), _full((1, D_MODEL)), _full((D_MODEL, IN_PAD))],
        out_specs=[row(D_MODEL)] + [row(w) for w in widths],
        out_shape=[jax.ShapeDtypeStruct((n, D_MODEL), F32)]
        + [jax.ShapeDtypeStruct((n, w), F32) for w in widths],
        compiler_params=_cparams(("parallel",)),
    )(x, g.reshape(1, D_MODEL), w_all)


def _matmul_kernel(a_ref, b_ref, o_ref):
    o_ref[...] = _mm(a_ref[...], b_ref[...])


def small_matmul(a, b):
    m, n = a.shape[0], b.shape[1]
    return pl.pallas_call(
        _matmul_kernel,
        out_shape=jax.ShapeDtypeStruct((m, n), F32),
        compiler_params=_cparams(None),
    )(a, b)


def _rwkv_kernel(has_vres, nb, c, t_valid, *refs):
    if has_vres:
        (prw_ref, sp_ref, s0_ref, vf_ref, mu_ref, w0_ref, w2_ref, a0_ref, a2_ref, g2_ref,
         v0_ref, v1_ref, v2_ref, kk_ref, ka_ref, rk_ref, lng_ref, lnb_ref,
         y_ref, s_ref, carry_ref) = refs
    else:
        (prw_ref, sp_ref, s0_ref, mu_ref, w0_ref, w2_ref, a0_ref, a2_ref, g2_ref,
         kk_ref, ka_ref, rk_ref, lng_ref, lnb_ref,
         y_ref, vout_ref, s_ref, carry_ref) = refs

    @pl.when(pl.program_id(1) == 0)
    def _():
        s_ref[...] = s0_ref[...]
        carry_ref[...] = sp_ref[...]

    n = nb * c
    rowc = _iota((c, 1), 0)
    rown = _iota((n, 1), 0)
    ti, si = _iota((c, c), 0), _iota((c, c), 1)
    tn, sn = _iota((n, n), 0), _iota((n, n), 1)
    tri = ((sn <= tn) & (sn // c == tn // c)).astype(BF16)
    e64 = _block_ones(RW_W, RW_HD)
    rows = lambda fn: jnp.concatenate([fn(bi) for bi in range(nb)], axis=0)

    p = rows(lambda bi: prw_ref[bi])
    prev = rows(lambda bi: jnp.where(rowc == 0, carry_ref[bi], pltpu.roll(prw_ref[bi], shift=1, axis=0)))
    for bi in range(nb):
        carry_ref[bi] = prw_ref[bi, c - 1:c, :]
    rw = p + (prev - p) * mu_ref[...]
    r, k, v = rw[:, 0:256], rw[:, 256:512], rw[:, 512:768]
    tail = rw[:, 768:896]
    wd, ad, gd = tail[:, 0:32], tail[:, 32:64], tail[:, 64:128]
    w_log = -_softplus(-(w0_ref[...] + _mm(jnp.tanh(wd), w2_ref[...]))) - 0.5
    logw = -jnp.exp(w_log)
    a = _sigmoid(a0_ref[...] + _mm(ad, a2_ref[...]))
    g = _mm(_sigmoid(gd), g2_ref[...])
    if has_vres:
        vf = rows(lambda bi: vf_ref[bi])
        v = v + (vf - v) * _sigmoid(v0_ref[...] + _mm(_mm(v, v1_ref[...]), v2_ref[...]))
    else:
        for bi in range(nb):
            vout_ref[bi] = v[bi * c:(bi + 1) * c]
    kk = k * kk_ref[...]
    kk = kk * lax.rsqrt(_mm2(kk * kk, e64) + 1e-12)
    k2 = k * (1.0 + (a - 1.0) * ka_ref[...])
    bv = kk * a
    if t_valid < c:
        valid = rown % c < t_valid
        logw = jnp.where(valid, logw, 0.0)
        kk = jnp.where(valid, kk, 0.0)
        bv = jnp.where(valid, bv, 0.0)
        k2 = jnp.where(valid, k2, 0.0)
        v = jnp.where(valid, v, 0.0)
    g_incl = _mm2l(tri, logw)
    g_last = rows(lambda bi: jnp.broadcast_to(g_incl[(bi + 1) * c - 1:(bi + 1) * c], (c, RW_W)))
    e_neg = jnp.exp(-g_incl)
    e_last = jnp.exp(g_last - g_incl)
    kq = kk * jnp.exp(g_incl - logw)
    rq = r * jnp.exp(g_incl)
    bd, kd = bv * e_neg, k2 * e_neg
    bl, kl = bv * e_last, k2 * e_last
    s_decay = jnp.exp(g_last)

    chains = [(bi, h) for bi in range(nb) for h in range(RW_HEADS)]
    cut = lambda t: [t[bi * c:(bi + 1) * c, h * RW_HD:(h + 1) * RW_HD] for bi, h in chains]
    kq_c, rq_c, bd_c, kd_c, bl_c, kl_c, v_c = cut(kq), cut(rq), cut(bd), cut(kd), cut(bl), cut(kl), cut(v)
    s0_c = [s_ref[bi, h] for bi, h in chains]
    stack = lambda xs, ys: [jnp.concatenate([x, y], axis=0) for x, y in zip(xs, ys)]
    bk_c = stack(bd_c, kd_c)
    kbk = [_mm(x, y, 1, 1) for x, y in zip(kq_c, bk_c)]
    rbk = [_mm(x, y, 1, 1) for x, y in zip(rq_c, bk_c)]
    a_b = [jnp.where(si < ti, x[:, 0:c], 0.0) for x in kbk]
    a_k = [jnp.where(si < ti, x[:, c:2 * c], 0.0) for x in kbk]
    m_b = [jnp.where(si <= ti, x[:, 0:c], 0.0) for x in rbk]
    m_k = [jnp.where(si <= ti, x[:, c:2 * c], 0.0) for x in rbk]
    krs0 = [_mm(x, s, 1, 1) for x, s in zip(stack(kq_c, rq_c), s0_c)]
    amv = [_mm(x, y) for x, y in zip(stack(a_k, m_k), v_c)]
    eye = (si == ti).astype(F32)
    t_inv = [eye - x for x in a_b]
    pw = a_b
    lvl = 2
    while lvl < c:
        pw = [_mm(x, x) for x in pw]
        t_inv = [x + _mm(x, y) for x, y in zip(t_inv, pw)]
        lvl *= 2
    u = [-_mm(x, y[0:c] + z[0:c]) for x, y, z in zip(t_inv, krs0, amv)]
    yh = [y[c:2 * c] + _mm(mb, uu) + z[c:2 * c] for y, mb, uu, z in zip(krs0, m_b, u, amv)]
    upd = [_mm(x, y, 0, 0) for x, y in zip(stack(u, v_c), stack(bl_c, kl_c))]
    for (bi, h), s0, dd in zip(chains, s0_c, upd):
        s_ref[bi, h] = s0 * s_decay[(bi + 1) * c - 1:(bi + 1) * c, h * RW_HD:(h + 1) * RW_HD] + dd

    y = rows(lambda bi: jnp.concatenate(yh[bi * RW_HEADS:(bi + 1) * RW_HEADS], axis=1))
    mean = _mm2(y, e64) * (1.0 / RW_HD)
    yc = y - mean
    var = _mm2(yc * yc, e64) * (1.0 / RW_HD)
    bonus = _mm2(r * k2 * rk_ref[...], e64) * v
    out = (yc * lax.rsqrt(var + RW_GN_EPS) * lng_ref[...] + lnb_ref[...] + bonus) * g
    for bi in range(nb):
        y_ref[bi] = out[bi * c:(bi + 1) * c]


def rwkv_mixer(prw, shift_proj, s0, v_first, p, t_valid):
    b, t, _ = prw.shape
    c = min(RW_CHUNK, t)
    nb = min(RW_BATCH * max(1, RW_CHUNK // (4 * c)), b)
    has_vres = v_first is not None
    tok = lambda w: pl.BlockSpec((nb, c, w), lambda i, j: (i, j, 0))
    per_b = lambda shape: pl.BlockSpec((nb,) + shape, lambda i, j: (i,) + (0,) * len(shape))
    vec = lambda a: a.reshape(1, -1)
    ins = [prw, shift_proj, s0]
    specs = [tok(RW_COLS), per_b((1, RW_COLS)), per_b((RW_HEADS, RW_HD, RW_HD))]
    if has_vres:
        ins.append(v_first)
        specs.append(tok(RW_W))
    params = [vec(p['rw_mu']), vec(p['rw_w0']), p['rw_w2'], vec(p['rw_a0']), p['rw_a2'], p['rw_g2']]
    if has_vres:
        params += [vec(p['rw_v0']), p['rw_v1'], p['rw_v2']]
    params += [vec(p['rw_k_k']), vec(p['rw_k_a']), vec(p['rw_r_k']), vec(p['rw_ln_g']), vec(p['rw_ln_b'])]
    ins += params
    specs += [_full(a.shape) for a in params]
    y_shape = jax.ShapeDtypeStruct((b, t, RW_W), F32)
    s_shape = jax.ShapeDtypeStruct((b, RW_HEADS, RW_HD, RW_HD), F32)
    s_spec = per_b((RW_HEADS, RW_HD, RW_HD))
    if has_vres:
        out_shape, out_specs = [y_shape, s_shape], [tok(RW_W), s_spec]
    else:
        out_shape, out_specs = [y_shape, y_shape, s_shape], [tok(RW_W), tok(RW_W), s_spec]
    outs = pl.pallas_call(
        functools.partial(_rwkv_kernel, has_vres, nb, c, t_valid),
        grid=(b // nb, t // c),
        in_specs=specs,
        out_specs=out_specs,
        out_shape=out_shape,
        scratch_shapes=[pltpu.VMEM((nb, 1, RW_COLS), F32)],
        compiler_params=_cparams(("parallel", "arbitrary")),
    )(*ins)
    if has_vres:
        return outs[0], v_first, outs[1]
    return outs[0], outs[1], outs[2]


def _gla_kernel(nb, c, sub, t_valid, pg_ref, s0_ref, a2_ref, ab_ref, ng_ref, y_ref, s_ref):
    @pl.when(pl.program_id(1) == 0)
    def _():
        s_ref[...] = s0_ref[...]

    expand_b = (_iota((GLA_KW, GLA_W), 0) // GLA_DK == _iota((GLA_KW, GLA_W), 1) // GLA_DV).astype(BF16)
    own_block = (_iota((GLA_W, GLA_KW), 0) // GLA_DV == _iota((GLA_W, GLA_KW), 1) // GLA_DK)
    e64 = _block_ones(GLA_W, GLA_DV)
    n = nb * c
    nsub = c // sub
    tn, sn = _iota((n, n), 0), _iota((n, n), 1)
    tri = ((sn <= tn) & (sn // sub == tn // sub)).astype(BF16)
    row = _iota((sub, 1), 0)
    rown = _iota((n, 1), 0)
    pg = jnp.concatenate([pg_ref[bi] for bi in range(nb)], axis=0)
    q = pg[:, 0:128] * (GLA_DK ** -0.5)
    k = pg[:, 128:256]
    v = pg[:, 256:512]
    rg = pg[:, 512:768]
    z = _mm(pg[:, 768:768 + GLA_A_LORA], a2_ref[...]) + ab_ref[...]
    la = (jnp.minimum(z, 0.0) - jnp.log(1.0 + jnp.exp(-jnp.abs(z)))) / GLA_TAU
    if t_valid < c:
        valid = rown % c < t_valid
        la = jnp.where(valid, la, 0.0)
        k = jnp.where(valid, k, 0.0)
    bcum = _mm2l(tri, la)
    qdec = q * jnp.exp(bcum)

    blocks = [(bi, j) for bi in range(nb) for j in range(nsub)]
    cut = lambda t: [t[bi * c + j * sub:bi * c + (j + 1) * sub] for bi, j in blocks]
    q_b, k_b, v_b, b_b, qd_b = cut(q), cut(k), cut(v), cut(bcum), cut(qdec)
    amat = []
    for qq, kk, bb in zip(q_b, k_b, b_b):
        prods = [jnp.where(row >= s, qq * kk[s:s + 1] * jnp.exp(jnp.minimum(bb - bb[s:s + 1], 0.0)), 0.0)
                 for s in range(sub)]
        amat.append(_mm(jnp.concatenate(prods, axis=0), expand_b))
    upd = [jnp.where(own_block, _mm(vv, kk * jnp.exp(bb[sub - 1:sub] - bb), 0, 0), 0.0)
           for vv, kk, bb in zip(v_b, k_b, b_b)]
    states = []
    for bi in range(nb):
        st = s_ref[bi]
        for j in range(nsub):
            states.append(st)
            i = bi * nsub + j
            st = st * jnp.exp(b_b[i][sub - 1:sub]) + upd[i]
        s_ref[bi] = st
    outs = []
    for i in range(len(blocks)):
        o = _mm(qd_b[i], states[i], 1, 1)
        for s in range(sub):
            o = o + amat[i][s * sub:(s + 1) * sub] * v_b[i][s:s + 1]
        outs.append(o)
    o = jnp.concatenate(outs, axis=0)
    ms = _mm2(o * o, e64) * (1.0 / GLA_DV)
    y = o * lax.rsqrt(ms + EPS) * ng_ref[...] * (rg * _sigmoid(rg))
    for bi in range(nb):
        y_ref[bi] = y[bi * c:(bi + 1) * c]


def gla_mixer(pg, s0t, p, t_valid):
    b, t, _ = pg.shape
    sub = min(GLA_SUB, t)
    c = min(4 * sub, t)
    nb = min(GLA_BATCH, b)
    tok = lambda w: pl.BlockSpec((nb, c, w), lambda i, j: (i, j, 0))
    st_spec = pl.BlockSpec((nb, GLA_W, GLA_KW), lambda i, j: (i, 0, 0))
    ng = jnp.tile(p['gla_norm_g'], GLA_HEADS).reshape(1, GLA_W)
    return pl.pallas_call(
        functools.partial(_gla_kernel, nb, c, sub, t_valid),
        grid=(b // nb, t // c),
        in_specs=[tok(GLA_PAD), st_spec, _full((GLA_A_LORA, GLA_KW)), _full((1, GLA_KW)), _full((1, GLA_W))],
        out_specs=[tok(GLA_W), st_spec],
        out_shape=[jax.ShapeDtypeStruct((b, t, GLA_W), F32), jax.ShapeDtypeStruct((b, GLA_W, GLA_KW), F32)],
        compiler_params=_cparams(("parallel", "arbitrary")),
    )(pg, s0t, p['gla_a2'], p['gla_ab'].reshape(1, GLA_KW), ng)


def _s5_kernel(rows, seg, carry_state, u_ref, hre_ref, him_ref, pre_ref, pim_ref, bre_ref, bim_ref,
               cre_ref, cim_ref, d_ref, ga_ref, gb_ref, y_ref, ore_ref, oim_ref, *scratch):
    u = u_ref[...]
    drive = _mm if carry_state else _mm3
    h_re = drive(u, bre_ref[...])
    h_im = drive(u, bim_ref[...])
    tpos = _iota((rows, 1), 0) % seg
    d = 1
    while d < seg:
        ar, ai = pre_ref[d - 1:d], pim_ref[d - 1:d]
        sr = jnp.where(tpos >= d, pltpu.roll(h_re, shift=d, axis=0), 0.0)
        sm = jnp.where(tpos >= d, pltpu.roll(h_im, shift=d, axis=0), 0.0)
        h_re, h_im = h_re + ar * sr - ai * sm, h_im + ar * sm + ai * sr
        d *= 2
    if carry_state:
        cre_s, cim_s = scratch

        @pl.when(pl.program_id(1) == 0)
        def _():
            cre_s[...] = hre_ref[0]
            cim_s[...] = him_ref[0]
        i_re, i_im = cre_s[...], cim_s[...]
        pr, pi = pre_ref[0:seg], pim_ref[0:seg]
        parts_re, parts_im = [], []
        for k in range(rows // seg):
            sl = slice(k * seg, (k + 1) * seg)
            p_re = h_re[sl] + pr * i_re - pi * i_im
            p_im = h_im[sl] + pr * i_im + pi * i_re
            i_re, i_im = p_re[seg - 1:seg], p_im[seg - 1:seg]
            parts_re.append(p_re)
            parts_im.append(p_im)
        h_re, h_im = jnp.concatenate(parts_re, axis=0), jnp.concatenate(parts_im, axis=0)
        cre_s[...] = i_re
        cim_s[...] = i_im
        ore_ref[0] = i_re
        oim_ref[0] = i_im
    else:
        i_re, i_im = hre_ref[...], him_ref[...]
        pr, pi = pre_ref[...], pim_ref[...]
        h_re, h_im = h_re + pr * i_re - pi * i_im, h_im + pr * i_im + pi * i_re
        ore_ref[...] = h_re
        oim_ref[...] = h_im
    y = _mm(h_re, cre_ref[...]) - _mm(h_im, cim_ref[...]) + d_ref[...] * u
    z = 0.5 * y * (1.0 + jnp.tanh(0.7978845608028654 * (y + 0.044715 * y * y * y)))
    y_ref[...] = _mm(z, ga_ref[...]) * _sigmoid(_mm(z, gb_ref[...]))


def s5_mixer(u, h0_re, h0_im, p, seq_len):
    n = u.shape[0]
    b = n // seq_len
    f32 = F32
    a_re, a_im = p['s5_a_re'], p['s5_a_im']
    dt = jnp.exp(p['s5_log_dt'])[:, None]
    mag = jnp.exp(a_re * dt)
    ab_re, ab_im = mag * jnp.cos(a_im * dt), mag * jnp.sin(a_im * dt)
    den = a_re * a_re + a_im * a_im
    cr = ((ab_re - 1.0) * a_re + ab_im * a_im) / den
    ci = (ab_im * a_re - (ab_re - 1.0) * a_im) / den
    bb_re = cr[..., None] * p['s5_b_re'] - ci[..., None] * p['s5_b_im']
    bb_im = cr[..., None] * p['s5_b_im'] + ci[..., None] * p['s5_b_re']
    eye_g = jnp.eye(S5_GROUPS, dtype=f32)

    def in_blockdiag(w):
        return jnp.einsum('gpi,gh->gihp', w, eye_g).reshape(S5_W, S5_STATE)

    def out_blockdiag(w):
        return jnp.einsum('gip,gh->gphi', w, eye_g).reshape(S5_STATE, S5_W)

    carry_state = seq_len > S5_TILE
    rows = S5_TILE
    seg = SUBLANE if carry_state else seq_len
    pw_re, pw_im = ab_re.reshape(1, S5_STATE), ab_im.reshape(1, S5_STATE)
    while pw_re.shape[0] < seg:
        tr, ti_ = pw_re[-1:], pw_im[-1:]
        pw_re, pw_im = (jnp.concatenate([pw_re, pw_re * tr - pw_im * ti_], axis=0),
                        jnp.concatenate([pw_im, pw_re * ti_ + pw_im * tr], axis=0))
    pw_re, pw_im = jnp.tile(pw_re, (rows // seg, 1)), jnp.tile(pw_im, (rows // seg, 1))
    weights = [pw_re, pw_im, in_blockdiag(bb_re), in_blockdiag(bb_im),
               out_blockdiag(p['s5_c_re']), out_blockdiag(p['s5_c_im']),
               p['s5_d'].reshape(1, S5_W), p['s5_glu_a'], p['s5_glu_b']]
    wspecs = [_full(w.shape) for w in weights]
    if carry_state:
        tiles = seq_len // rows
        grid = (b, tiles)
        tok = lambda w: pl.BlockSpec((rows, w), lambda i, j: (i * tiles + j, 0))
        st_in = pl.BlockSpec((1, 1, S5_STATE), lambda i, j: (i, 0, 0))
        ins = [u, h0_re.reshape(b, 1, S5_STATE), h0_im.reshape(b, 1, S5_STATE)]
        st_shape = jax.ShapeDtypeStruct((b, 1, S5_STATE), f32)
        scratch = [pltpu.VMEM((1, S5_STATE), f32)] * 2
        sem = ("parallel", "arbitrary")
        st_out = st_in
    else:
        grid = (n // rows,)
        tok = lambda w: pl.BlockSpec((rows, w), lambda i: (i, 0))
        st_in = tok(S5_STATE)
        ins = [u, jnp.repeat(h0_re, seq_len, axis=0), jnp.repeat(h0_im, seq_len, axis=0)]
        st_shape = jax.ShapeDtypeStruct((n, S5_STATE), f32)
        scratch = []
        sem = ("parallel",)
        st_out = st_in
    y, o_re, o_im = pl.pallas_call(
        functools.partial(_s5_kernel, rows, seg, carry_state),
        grid=grid,
        in_specs=[tok(S5_W), st_in, st_in] + wspecs,
        out_specs=[tok(S5_W), st_out, st_out],
        out_shape=[jax.ShapeDtypeStruct((n, S5_W), f32), st_shape, st_shape],
        scratch_shapes=scratch,
        compiler_params=_cparams(sem),
    )(*ins, *weights)
    if carry_state:
        return y, o_re.reshape(b, S5_STATE), o_im.reshape(b, S5_STATE)
    last = lambda h: h.reshape(b, seq_len, S5_STATE)[:, -1]
    return y, last(o_re), last(o_im)


def _rope_rotate(x, cos_t, sin_a, sin_b):
    w = x.shape[1]
    return x * cos_t + pltpu.roll(x, shift=w - MLA_ROPE // 2, axis=1) * sin_a + pltpu.roll(x, shift=MLA_ROPE // 2, axis=1) * sin_b


def _mla_keys(c, kr, cos_t, sin_a, sin_b, wuk_pad, place):
    k_nope = _mm(c, wuk_pad)
    k_rope = _rope_rotate(_mm3(kr, place), cos_t, sin_a, sin_b)
    return k_nope, k_rope


def _mla_prep_kernel(pm_ref, cos_ref, sina_ref, sinb_ref, qag_ref, wuq_ref, qg_ref, kvag_ref,
                     wuk_ref, wuv_ref, pk_ref, q_ref, k_ref, v_ref, c_ref, kr_ref):
    pm = pm_ref[...]
    qd, kvd, kr = pm[:, 0:256], pm[:, 256:512], pm[:, 512:512 + MLA_ROPE]
    tile4 = lambda t: jnp.concatenate([t] * MLA_HEADS, axis=1)
    cos_t, sin_a, sin_b = tile4(cos_ref[...]), tile4(sina_ref[...]), tile4(sinb_ref[...])
    e128 = _block_ones(MLA_HW, MLA_HP)
    qn = qd * lax.rsqrt(jnp.mean(qd * qd, axis=-1, keepdims=True) + EPS) * qag_ref[...]
    q = _mm(qn, wuq_ref[...])
    q = q * lax.rsqrt(_mm2(q * q, e128) * (1.0 / MLA_QK) + EPS) * qg_ref[...]
    q = _rope_rotate(q, cos_t, sin_a, sin_b)
    c = kvd * lax.rsqrt(jnp.mean(kvd * kvd, axis=-1, keepdims=True) + EPS) * kvag_ref[...]
    c_ref[...] = c
    kr_ref[...] = kr
    k_nope, k_rope = _mla_keys(c, kr, cos_t, sin_a, sin_b, wuk_ref[...], pk_ref[...])
    ssq = _mm2(k_nope * k_nope, e128) + jnp.sum(kr * kr, axis=-1, keepdims=True)
    kfull = (k_nope + k_rope) * (lax.rsqrt(ssq * (1.0 / MLA_QK) + EPS) * MLA_SCALE)
    vfull = _mm(c, wuv_ref[...])
    for h in range(MLA_HEADS):
        hs = slice(h * MLA_HP, (h + 1) * MLA_HP)
        q_ref[h] = _bf(q[:, hs])
        k_ref[h] = _bf(kfull[:, hs])
        v_ref[h] = _bf(vfull[:, hs])


def _flash_kernel(tq, tk, q_ref, k_ref, v_ref, o_ref):
    iq = pl.program_id(2)
    q = q_ref[0]
    ratio = tq // tk

    def step(ik, carry, masked):
        m, l, acc = carry
        off = pl.multiple_of(ik * tk, tk)
        s = _dg(q, k_ref[0, pl.ds(off, tk), :], 1, 1)
        if masked:
            s = jnp.where(off + _iota((tq, tk), 1) <= iq * tq + _iota((tq, tk), 0), s, NEG_INF)
        m_new = jnp.maximum(m, jnp.max(s, axis=-1, keepdims=True))
        alpha = jnp.exp(m - m_new)
        pexp = jnp.exp(s - m_new)
        l = alpha * l + jnp.sum(pexp, axis=-1, keepdims=True)
        acc = alpha * acc + jnp.dot(_bf(pexp), v_ref[0, pl.ds(off, tk), :], preferred_element_type=F32)
        return m_new, l, acc

    carry = (jnp.full((tq, 1), NEG_INF, F32), jnp.zeros((tq, 1), F32), jnp.zeros((tq, MLA_HP), F32))
    carry = lax.fori_loop(0, iq * ratio, lambda ik, cr: step(ik, cr, False), carry)
    for d in range(ratio):
        carry = step(iq * ratio + d, carry, True)
    m, l, acc = carry
    o_ref[...] = acc / l


def mla_prompt(pm, p, batch, seq_len, tables):
    n = pm.shape[0]
    rows = min(ROW_TILE, seq_len)
    tiles = seq_len // rows
    cos_t, sin_a, sin_b = tables
    tok = lambda w: pl.BlockSpec((rows, w), lambda i: (i, 0))
    tab = pl.BlockSpec((rows, MLA_HP), lambda i: (i % tiles, 0))
    hd = pl.BlockSpec((MLA_HEADS, rows, MLA_HP), lambda i: (0, i, 0))
    weights = [p['qag'], p['wuq_pad'], p['qg_pad'], p['kvag'], p['wuk_pad'], p['wuv_pad'], p['pk']]
    hshape = jax.ShapeDtypeStruct((MLA_HEADS, n, MLA_HP), BF16)
    q, k, v, c, kr = pl.pallas_call(
        _mla_prep_kernel,
        grid=(n // rows,),
        in_specs=[tok(MLA_PAD), tab, tab, tab] + [_full(w.shape) for w in weights],
        out_specs=[hd, hd, hd, tok(MLA_LORA), tok(MLA_ROPE)],
        out_shape=[hshape, hshape, hshape, jax.ShapeDtypeStruct((n, MLA_LORA), F32),
                   jax.ShapeDtypeStruct((n, MLA_ROPE), F32)],
        compiler_params=_cparams(("parallel",)),
    )(pm, cos_t, sin_a, sin_b, *weights)
    tq = min(ATT_TQ, seq_len)
    tk = min(ATT_TK, tq)
    nq = seq_len // tq
    kv_spec = pl.BlockSpec((1, seq_len, MLA_HP), lambda b, h, i: (h, b, 0))
    o = pl.pallas_call(
        functools.partial(_flash_kernel, tq, tk),
        grid=(batch, MLA_HEADS, nq),
        in_specs=[pl.BlockSpec((1, tq, MLA_HP), lambda b, h, i: (h, b * nq + i, 0)), kv_spec, kv_spec],
        out_specs=pl.BlockSpec((tq, MLA_HP), lambda b, h, i: (b * nq + i, h)),
        out_shape=jax.ShapeDtypeStruct((n, MLA_HW), F32),
        compiler_params=_cparams(("parallel", "parallel", "arbitrary")),
    )(q, k, v)
    return o, c, kr


def _mla_decode_kernel(n_pages, single_step, t_new, *refs):
    pm_ref, tabn_ref, tabp_ref, vec_ref, w3_ref, pkg_ref, wuk_ref = refs[1:8]
    c_refs = refs[8:8 + n_pages]
    kr_refs = refs[8 + n_pages:8 + 2 * n_pages]
    o_ref, c_out, kr_out, q_s, qn_s, qr_s, m_s, l_s, acc_s, cbf_s = refs[8 + 2 * n_pages:]
    qag, kvag, qg = vec_ref[0:1, 0:MLA_LORA], vec_ref[1:2, 0:MLA_LORA], vec_ref[2:3, :]
    place, kgain = pkg_ref[:, 0:MLA_HW], pkg_ref[:, MLA_HW:MLA_HW + 1]
    j = pl.program_id(1)
    nq = MLA_HEADS * SUBLANE
    n = n_pages * PAGE_SIZE
    half = MLA_ROPE // 2
    tile4 = lambda t: jnp.concatenate([t] * MLA_HEADS, axis=1)
    row_head = (_iota((nq, MLA_HW), 0) // SUBLANE == _iota((nq, MLA_HW), 1) // MLA_HP)
    head_sel = (_iota((nq, MLA_LORA), 0) // SUBLANE == _iota((nq, MLA_LORA), 1) // MLA_NOPE)

    def softmax_step(s, values):
        m_old = m_s[...]
        m_new = jnp.maximum(m_old, jnp.max(s, axis=-1, keepdims=True))
        alpha = jnp.exp(m_old - m_new)
        pexp = jnp.exp(s - m_new)
        l_s[...] = alpha * l_s[...] + jnp.sum(pexp, axis=-1, keepdims=True)
        m_s[...] = m_new
        acc_s[...] = acc_s[...] * alpha + _mm(pexp, values)

    pm = pm_ref[0]
    qd, kvd, kr_new = pm[:, 0:256], pm[:, 256:512], pm[:, 512:512 + MLA_ROPE]
    c_new = kvd * lax.rsqrt(jnp.mean(kvd * kvd, axis=-1, keepdims=True) + EPS) * kvag
    cos_n, sin_an, sin_bn = (tile4(tabn_ref[:, 0:MLA_HP]), tile4(tabn_ref[:, MLA_HP:2 * MLA_HP]),
                             tile4(tabn_ref[:, 2 * MLA_HP:3 * MLA_HP]))

    def when(cond):
        return (lambda fn: fn()) if single_step else pl.when(cond)

    @when(j == 0)
    def _():
        e128 = _block_ones(MLA_HW, MLA_HP)
        qn = qd * lax.rsqrt(jnp.mean(qd * qd, axis=-1, keepdims=True) + EPS) * qag
        q = _mm(qn, w3_ref[:, 0:MLA_HW])
        q = q * lax.rsqrt(_mm2(q * q, e128) * (1.0 / MLA_QK) + EPS) * qg
        q = _rope_rotate(q, cos_n, sin_an, sin_bn)
        q4 = jnp.concatenate([q] * MLA_HEADS, axis=0)
        q_s[...] = jnp.where(row_head, q4, 0.0)
        li, lo = _iota((MLA_HW, MLA_LORA), 0), _iota((MLA_HW, MLA_LORA), 1)
        take_nope = ((li // MLA_HP == lo // MLA_NOPE) & (li % MLA_HP == lo % MLA_NOPE)).astype(BF16)
        qn_s[...] = _bf(jnp.where(head_sel, _mm(q4, take_nope), 0.0))
        ri, ro = _iota((MLA_HW, MLA_ROPE), 0), _iota((MLA_HW, MLA_ROPE), 1)
        take_rope = (ri % MLA_HP == MLA_NOPE + ro).astype(BF16)
        qr_s[...] = _bf(_mm(jnp.where(row_head, q4, 0.0), take_rope))
        m_s[...] = jnp.full(m_s.shape, NEG_INF, F32)
        l_s[...] = jnp.zeros(l_s.shape, F32)
        acc_s[...] = jnp.zeros(acc_s.shape, F32)
        c_out[0] = c_new
        kr_out[0] = kr_new

    for g in range(n_pages):
        cbf_s[g * PAGE_SIZE:(g + 1) * PAGE_SIZE, :] = _bf(c_refs[g][0, 0])
    kt = jnp.concatenate([kr_refs[g][0, 0] for g in range(n_pages)], axis=1)
    sq_rope = jnp.sum(kt * kt, axis=0, keepdims=True)
    kg = kt * kgain
    x1, x2 = kg[0:half], kg[half:]
    cos_p, sin_p = tabp_ref[0:half, :], tabp_ref[half:MLA_ROPE, :]
    krot = jnp.concatenate([x1 * cos_p - x2 * sin_p, x2 * cos_p + x1 * sin_p], axis=0)
    s_rope = jnp.dot(qr_s[...], _bf(krot), preferred_element_type=F32)
    nt = n // DEC_SUB
    sel_b = _bf(head_sel.astype(F32))
    lanes = lambda t, i: t[:, i * DEC_SUB:(i + 1) * DEC_SUB]
    val = [dict() for _ in range(nt)]

    def stage(k, i):
        d = val[i]
        if k == 0:
            d['cb'] = cbf_s[i * DEC_SUB:(i + 1) * DEC_SUB, :]
            d['kpre'] = jnp.dot(d['cb'], wuk_ref[...], preferred_element_type=F32)
        elif k == 1:
            d['ksq'], d['kb'] = _bf(d['kpre'] * d['kpre']), _bf(d['kpre'])
        elif k == 2:
            d['ssq'] = _dg(sel_b, d['ksq'], 1, 1)
            d['sc'] = _dg(qn_s[...], d['kb'], 1, 1)
        elif k == 3:
            norm = lax.rsqrt((d['ssq'] + lanes(sq_rope, i)) * (1.0 / MLA_QK) + EPS) * MLA_SCALE
            d['s'] = (d['sc'] + lanes(s_rope, i)) * norm
            d['m'] = jnp.max(d['s'], axis=-1, keepdims=True)
        elif k == 4:
            pexp = jnp.exp(d['s'] - d['m'])
            d['l'] = jnp.sum(pexp, axis=-1, keepdims=True)
            d['acc'] = jnp.dot(_bf(pexp), d['cb'], preferred_element_type=F32)

    n_stage = 5
    for t in range(nt + n_stage - 1):
        for k in range(n_stage):
            if 0 <= t - k < nt:
                stage(k, t - k)
    m_old = m_s[...]
    m_new = m_old
    for d in val:
        m_new = jnp.maximum(m_new, d['m'])
    alpha = jnp.exp(m_old - m_new)
    l_new = alpha * l_s[...]
    acc = acc_s[...] * alpha
    for d in val:
        w = jnp.exp(d['m'] - m_new)
        l_new = l_new + w * d['l']
        acc = acc + w * d['acc']
    m_s[...] = m_new
    l_s[...] = l_new
    acc_s[...] = acc

    @when(j == pl.num_programs(1) - 1)
    def _():
        k_nope, k_rope = _mla_keys(c_new, kr_new, cos_n, sin_an, sin_bn, w3_ref[:, MLA_HW:2 * MLA_HW], place)
        ssq_n = _mm2l(row_head.astype(BF16), k_nope * k_nope, 1, 1)
        ssq_n = ssq_n + _mm2l(jnp.ones((SUBLANE, MLA_ROPE), BF16), kr_new * kr_new, 1, 1)[0:1]
        s_n = _mm(q_s[...], k_nope + k_rope, 1, 1) * (lax.rsqrt(ssq_n * (1.0 / MLA_QK) + EPS) * MLA_SCALE)
        qry_t = _iota((nq, SUBLANE), 0) % SUBLANE
        key_t = _iota((nq, SUBLANE), 1)
        softmax_step(jnp.where((key_t <= qry_t) & (key_t < t_new), s_n, NEG_INF), c_new)
        o_lat = acc_s[...] / l_s[...]
        full = jnp.where(row_head, _mm(o_lat, w3_ref[:, 2 * MLA_HW:3 * MLA_HW]), 0.0)
        gather = (_iota((SUBLANE, nq), 0) == _iota((SUBLANE, nq), 1) % SUBLANE).astype(BF16)
        o_ref[0] = _mm2l(gather, full)


def mla_decode(pm, cache_c, cache_krt, page_table, layer, p, tables_new, tables_past_t, t_new):
    b = pm.shape[0]
    pages = page_table.shape[1]
    steps = pages // DEC_PAGES
    n = DEC_PAGES * PAGE_SIZE
    per_b = lambda w: pl.BlockSpec((1, SUBLANE, w), lambda i, j, pt: (i, 0, 0))
    tab_new = pl.BlockSpec((SUBLANE, 3 * MLA_HP), lambda i, j, pt: (0, 0))
    tab_past = pl.BlockSpec((MLA_ROPE, n), lambda i, j, pt: (0, j))
    pad_row = lambda v: jnp.pad(v, ((0, 0), (0, MLA_HW - v.shape[1])))
    vecs = jnp.pad(jnp.concatenate([pad_row(p['qag']), pad_row(p['kvag']), p['qg_pad']], axis=0),
                   ((0, SUBLANE - 3), (0, 0)))
    w3 = jnp.concatenate([p['wuq_pad'], p['wuk_pad'], p['wuv_pad']], axis=1)
    pkg = jnp.concatenate([p['pk'], p['kgain_col'], jnp.zeros((MLA_ROPE, LANE - 1), F32)], axis=1)
    weights = [vecs, w3, pkg, p['wuk']]
    wspecs = [pl.BlockSpec(w.shape, lambda i, j, pt, nd=w.ndim: (0,) * nd) for w in weights]

    def page_spec(rows, w, slot):
        return pl.BlockSpec((1, 1, rows, w), lambda i, j, pt: (layer, pt[i, j * DEC_PAGES + slot], 0, 0))

    c_specs = [page_spec(PAGE_SIZE, MLA_LORA, s) for s in range(DEC_PAGES)]
    kr_specs = [page_spec(MLA_ROPE, PAGE_SIZE, s) for s in range(DEC_PAGES)]
    nq = MLA_HEADS * SUBLANE
    grid_spec = pltpu.PrefetchScalarGridSpec(
        num_scalar_prefetch=1,
        grid=(b, steps),
        in_specs=[per_b(MLA_PAD), tab_new, tab_past] + wspecs + c_specs + kr_specs,
        out_specs=[per_b(MLA_HW), per_b(MLA_LORA), per_b(MLA_ROPE)],
        scratch_shapes=[pltpu.VMEM((nq, MLA_HW), F32), pltpu.VMEM((nq, MLA_LORA), BF16),
                        pltpu.VMEM((nq, MLA_ROPE), BF16), pltpu.VMEM((nq, 1), F32),
                        pltpu.VMEM((nq, 1), F32), pltpu.VMEM((nq, MLA_LORA), F32),
                        pltpu.VMEM((n, MLA_LORA), BF16)],
    )
    o, c, kr = pl.pallas_call(
        functools.partial(_mla_decode_kernel, DEC_PAGES, steps == 1, t_new),
        grid_spec=grid_spec,
        out_shape=[jax.ShapeDtypeStruct((b, SUBLANE, MLA_HW), F32),
                   jax.ShapeDtypeStruct((b, SUBLANE, MLA_LORA), F32),
                   jax.ShapeDtypeStruct((b, SUBLANE, MLA_ROPE), F32)],
        compiler_params=_cparams(("parallel", "arbitrary")),
    )(page_table, pm, jnp.concatenate(tables_new, axis=1), jnp.concatenate(tables_past_t, axis=0), *weights,
      *([cache_c] * DEC_PAGES), *([cache_krt] * DEC_PAGES))
    return o, c, kr


def rope_tables_transposed(pos):
    half = MLA_ROPE // 2
    inv = ROPE_THETA ** (-jnp.arange(half, dtype=F32) / half)
    ang = inv[:, None] * pos.astype(F32)[None, :]
    return jnp.cos(ang), jnp.sin(ang)


def rope_tables(pos):
    half = MLA_ROPE // 2
    inv = ROPE_THETA ** (-jnp.arange(half, dtype=F32) / half)
    ang = pos.astype(F32)[:, None] * inv
    cos, sin = jnp.cos(ang), jnp.sin(ang)
    n = pos.shape[0]
    ones, zeros = jnp.ones((n, MLA_NOPE), F32), jnp.zeros((n, MLA_NOPE), F32)
    zpad = jnp.zeros((n, MLA_HP - MLA_QK), F32)
    zh = jnp.zeros((n, half), F32)
    cos_t = jnp.concatenate([ones, cos, cos, zpad + 1.0], axis=1)
    sin_a = jnp.concatenate([zeros, -sin, zh, zpad], axis=1)
    sin_b = jnp.concatenate([zeros, zh, sin, zpad], axis=1)
    return cos_t, sin_a, sin_b


def _merge_kernel(x_ref, xn_ref, yrw_ref, ygla_ref, ys5_ref, ymla_ref, wg_ref, wb_ref, wbm_ref, wo_ref, o_ref):
    xb = _bf(xn_ref[...])
    merged = None
    for n, y_ref in enumerate((yrw_ref, ygla_ref, ys5_ref, ymla_ref)):
        gate = _sigmoid(jnp.dot(xb, wg_ref[:, n * D_MODEL:(n + 1) * D_MODEL], preferred_element_type=F32))
        w = wbm_ref[...] if n == 3 else wb_ref[n]
        term = gate * jnp.dot(_bf(y_ref[...]), w, preferred_element_type=F32)
        merged = term if merged is None else merged + term
    o_ref[...] = x_ref[...] + jnp.dot(_bf(merged), wo_ref[...], preferred_element_type=F32)


def merge(x, xn, y_rw, y_gla, y_s5, y_mla, p):
    n = x.shape[0]
    row = lambda w: pl.BlockSpec((ROW_TILE, w), lambda i: (i, 0))
    weights = [p['w_gate'], p['w_branch3'], p['w_branch_mla'], p['w_out']]
    return pl.pallas_call(
        _merge_kernel,
        grid=(n // ROW_TILE,),
        in_specs=[row(D_MODEL), row(D_MODEL), row(BRANCH_W), row(BRANCH_W), row(BRANCH_W), row(MLA_HW)]
        + [_full(w.shape) for w in weights],
        out_specs=row(D_MODEL),
        out_shape=jax.ShapeDtypeStruct((n, D_MODEL), F32),
        compiler_params=_cparams(("parallel",)),
    )(x, xn, y_rw, y_gla, y_s5, y_mla, *weights)


def _mlp_kernel(ff_chunk, x_ref, g_ref, w1_ref, w2_ref, o_ref):
    x = x_ref[...]
    xb = _bf(x * lax.rsqrt(jnp.mean(x * x, axis=-1, keepdims=True) + EPS) * g_ref[...])
    acc = x
    for o in range(0, D_FF, ff_chunk):
        h = jnp.maximum(jnp.dot(xb, w1_ref[:, o:o + ff_chunk], preferred_element_type=F32), 0.0)
        acc = acc + jnp.dot(_bf(h * h), w2_ref[o:o + ff_chunk, :], preferred_element_type=F32)
    o_ref[...] = acc


def mlp(x, p):
    n = x.shape[0]
    row = pl.BlockSpec((ROW_TILE, D_MODEL), lambda i: (i, 0))
    return pl.pallas_call(
        functools.partial(_mlp_kernel, 1024),
        grid=(n // ROW_TILE,),
        in_specs=[row, _full((1, D_MODEL)), _full((D_MODEL, D_FF)), _full((D_FF, D_MODEL))],
        out_specs=row,
        out_shape=jax.ShapeDtypeStruct((n, D_MODEL), F32),
        compiler_params=_cparams(("parallel",)),
    )(x, p['norm2_g'].reshape(1, D_MODEL), p['w_ff1'], p['w_ff2'])


def _pad_cols(w, width):
    return jnp.pad(w, ((0, 0), (0, width - w.shape[1])))


def _head_pad(w, used):
    kdim = w.shape[0]
    w = w.reshape(kdim, MLA_HEADS, used)
    return jnp.pad(w, ((0, 0), (0, 0), (0, MLA_HP - used))).reshape(kdim, MLA_HW)


def prepare_layer(l, W):
    p = {}
    w_in = W['w_in'][l]
    o_gla, o_s5, o_mla = RW_COLS, RW_COLS + GLA_COLS, RW_COLS + GLA_COLS + S5_W
    w_gla = w_in[:, o_gla:o_s5]
    w_gla = jnp.concatenate([w_gla[:, 0:512], w_gla[:, 528:784], w_gla[:, 512:528]], axis=1)
    p['w_rw'] = _bf(w_in[:, :RW_COLS])
    p['w_all'] = _bf(jnp.concatenate([w_in[:, :RW_COLS], _pad_cols(w_gla, GLA_PAD), w_in[:, o_s5:o_mla],
                                      _pad_cols(w_in[:, o_mla:], MLA_PAD)], axis=1))
    for name in ('norm1_g', 'rw_mu', 'rw_w0', 'rw_w2', 'rw_a0', 'rw_a2', 'rw_g2', 'rw_k_k', 'rw_k_a',
                 'rw_ln_g', 'rw_ln_b', 'gla_a2', 'gla_ab', 'gla_norm_g', 's5_a_re', 's5_a_im', 's5_log_dt',
                 's5_b_re', 's5_b_im', 's5_c_re', 's5_c_im', 's5_glu_a', 's5_glu_b', 'norm2_g'):
        p[name] = W[name][l]
    p['rw_r_k'] = W['rw_r_k'][l].reshape(RW_W)
    p['s5_d'] = W['s5_d'][l].reshape(S5_W)
    if l > 0:
        for name in ('rw_v0', 'rw_v1', 'rw_v2'):
            p[name] = W[name][l - 1]
    k_g = W['mla_k_g'][l]
    p['qag'] = W['mla_qa_g'][l].reshape(1, MLA_LORA)
    p['kvag'] = W['mla_kva_g'][l].reshape(1, MLA_LORA)
    p['wuq_pad'] = _head_pad(W['mla_w_uq'][l], MLA_QK)
    qg = W['mla_q_g'][l] * jnp.concatenate([k_g[:MLA_NOPE], jnp.ones((MLA_ROPE,), F32)])
    p['qg_pad'] = jnp.tile(jnp.pad(qg, (0, MLA_HP - MLA_QK)), MLA_HEADS).reshape(1, MLA_HW)
    p['wuk_pad'] = _head_pad(W['mla_w_uk'][l].reshape(MLA_LORA, MLA_HEADS * MLA_NOPE), MLA_NOPE)
    wuv = W['mla_w_uv'][l].reshape(MLA_LORA, MLA_HEADS * MLA_V)
    p['wuv_pad'] = _head_pad(wuv, MLA_V)
    p['wuk'] = _bf(W['mla_w_uk'][l].reshape(MLA_LORA, MLA_HEADS * MLA_NOPE))
    p['kgain_col'] = k_g[MLA_NOPE:].reshape(MLA_ROPE, 1)
    place = jnp.pad(jnp.diag(k_g[MLA_NOPE:]), ((0, 0), (MLA_NOPE, MLA_HP - MLA_QK)))
    p['pk'] = jnp.tile(place, (1, MLA_HEADS))
    p['w_gate'] = _bf(W['w_gate'][l])
    wb = W['w_branch'][l]
    p['w_branch3'] = _bf(wb)
    wbm = wb[3].reshape(MLA_HEADS, MLA_V, D_MODEL)
    p['w_branch_mla'] = _bf(jnp.pad(wbm, ((0, 0), (0, MLA_HP - MLA_V), (0, 0))).reshape(MLA_HW, D_MODEL))
    p['w_out'] = _bf(W['w_out'][l])
    p['w_ff1'] = _bf(W['w_ff1'][l])
    p['w_ff2'] = _bf(W['w_ff2'][l])
    return p


def _gla_state_in(s):
    eye = jnp.eye(GLA_HEADS, dtype=F32)
    return jnp.einsum('bhdv,hg->bhvgd', s, eye).reshape(s.shape[0], GLA_W, GLA_KW)


def _gla_state_out(st):
    b = st.shape[0]
    st = st.reshape(b, GLA_HEADS, GLA_DV, GLA_HEADS, GLA_DK)
    return jnp.stack([st[:, h, :, h, :] for h in range(GLA_HEADS)], axis=1).swapaxes(2, 3)


def kernel(x_prompt, x_sample, cache_mla_latent, cache_mla_krope, page_table, state_rwkv, state_shift, state_gla, state_s5_re, state_s5_im, norm1_g, w_in, rw_mu, rw_w0, rw_w2, rw_a0, rw_a2, rw_g2, rw_v0, rw_v1, rw_v2, rw_k_k, rw_k_a, rw_r_k, rw_ln_g, rw_ln_b, gla_a2, gla_ab, gla_norm_g, s5_a_re, s5_a_im, s5_log_dt, s5_b_re, s5_b_im, s5_c_re, s5_c_im, s5_d, s5_glu_a, s5_glu_b, mla_qa_g, mla_w_uq, mla_kva_g, mla_w_uk, mla_w_uv, mla_q_g, mla_k_g, w_gate, w_branch, w_out, norm2_g, w_ff1, w_ff2):
    W = dict(norm1_g=norm1_g, w_in=w_in, rw_mu=rw_mu, rw_w0=rw_w0, rw_w2=rw_w2, rw_a0=rw_a0, rw_a2=rw_a2,
             rw_g2=rw_g2, rw_v0=rw_v0, rw_v1=rw_v1, rw_v2=rw_v2, rw_k_k=rw_k_k, rw_k_a=rw_k_a, rw_r_k=rw_r_k,
             rw_ln_g=rw_ln_g, rw_ln_b=rw_ln_b, gla_a2=gla_a2, gla_ab=gla_ab, gla_norm_g=gla_norm_g,
             s5_a_re=s5_a_re, s5_a_im=s5_a_im, s5_log_dt=s5_log_dt, s5_b_re=s5_b_re, s5_b_im=s5_b_im,
             s5_c_re=s5_c_re, s5_c_im=s5_c_im, s5_d=s5_d, s5_glu_a=s5_glu_a, s5_glu_b=s5_glu_b,
             mla_qa_g=mla_qa_g, mla_w_uq=mla_w_uq, mla_kva_g=mla_kva_g, mla_w_uk=mla_w_uk, mla_w_uv=mla_w_uv,
             mla_q_g=mla_q_g, mla_k_g=mla_k_g, w_gate=w_gate, w_branch=w_branch, w_out=w_out,
             norm2_g=norm2_g, w_ff1=w_ff1, w_ff2=w_ff2)
    bp, tp, _ = x_prompt.shape
    bs, ts, _ = x_sample.shape
    depth = w_in.shape[0]
    np_, ns = bp * tp, bs * ts
    past_len = page_table.shape[1] * PAGE_SIZE
    tpad = SUBLANE

    tables_p = rope_tables(jnp.arange(tp, dtype=jnp.int32))
    tables_new = rope_tables(past_len + jnp.arange(tpad, dtype=jnp.int32))
    tables_past = rope_tables_transposed(jnp.arange(past_len, dtype=jnp.int32))
    cache_krt = jnp.swapaxes(cache_mla_krope, 2, 3)

    xp, xs = x_prompt.reshape(np_, D_MODEL), x_sample.reshape(ns, D_MODEL)
    pad_t = lambda a: jnp.pad(a.reshape(bs, ts, a.shape[-1]), ((0, 0), (0, tpad - ts), (0, 0)))
    unpad = lambda a: a[:, :ts].reshape(ns, a.shape[-1])
    vf_p = vf_s = None
    outs = {k: [] for k in ('c_p', 'kr_p', 'c_s', 'kr_s', 'rw_p', 'rw_s', 'sh_p', 'sh_s', 'gla_p', 'gla_s',
                            're_p', 'im_p', 're_s', 'im_s')}
    zeros_p = lambda *shape: jnp.zeros((bp,) + shape, F32)
    for l in range(depth):
        p = prepare_layer(l, W)
        xn_p, prw_p, pgla_p, ps5_p, pmla_p = norm_proj(xp, p['norm1_g'], p['w_all'])
        xn_s, prw_s, pgla_s, ps5_s, pmla_s = norm_proj(xs, p['norm1_g'], p['w_all'])
        outs['sh_p'].append(xn_p.reshape(bp, tp, D_MODEL)[:, -1])
        outs['sh_s'].append(xn_s.reshape(bs, ts, D_MODEL)[:, -1])

        y_rw_p, vf_p, s_rw_p = rwkv_mixer(prw_p.reshape(bp, tp, RW_COLS), zeros_p(1, RW_COLS),
                                          zeros_p(RW_HEADS, RW_HD, RW_HD), vf_p, p, tp)
        shift_proj = small_matmul(state_shift[l], p['w_rw']).reshape(bs, 1, RW_COLS)
        y_rw_s, vf_s, s_rw_s = rwkv_mixer(pad_t(prw_s), shift_proj, state_rwkv[l], vf_s, p, ts)
        outs['rw_p'].append(s_rw_p)
        outs['rw_s'].append(s_rw_s)

        y_gla_p, s_gla_p = gla_mixer(pgla_p.reshape(bp, tp, GLA_PAD), zeros_p(GLA_W, GLA_KW), p, tp)
        y_gla_s, s_gla_s = gla_mixer(pad_t(pgla_s), _gla_state_in(state_gla[l]), p, ts)
        outs['gla_p'].append(_gla_state_out(s_gla_p))
        outs['gla_s'].append(_gla_state_out(s_gla_s))

        y_s5_p, re_p, im_p = s5_mixer(ps5_p, zeros_p(S5_STATE), zeros_p(S5_STATE), p, tp)
        y_s5_s, re_s, im_s = s5_mixer(ps5_s, state_s5_re[l].reshape(bs, S5_STATE),
                                      state_s5_im[l].reshape(bs, S5_STATE), p, ts)
        for key, val, bb in (('re_p', re_p, bp), ('im_p', im_p, bp), ('re_s', re_s, bs), ('im_s', im_s, bs)):
            outs[key].append(val.reshape(bb, S5_GROUPS, S5_P))

        y_mla_p, c_p, kr_p = mla_prompt(pmla_p, p, bp, tp, tables_p)
        y_mla_s, c_s, kr_s = mla_decode(pad_t(pmla_s), cache_mla_latent, cache_krt, page_table, l, p,
                                        tables_new, tables_past, ts)
        outs['c_p'].append(c_p.reshape(bp, tp, MLA_LORA))
        outs['kr_p'].append(kr_p.reshape(bp, tp, MLA_ROPE))
        outs['c_s'].append(c_s[:, :ts])
        outs['kr_s'].append(kr_s[:, :ts])

        xp = mlp(merge(xp, xn_p, y_rw_p.reshape(np_, RW_W), y_gla_p.reshape(np_, GLA_W), y_s5_p, y_mla_p, p), p)
        xs = mlp(merge(xs, xn_s, unpad(y_rw_s), unpad(y_gla_s), y_s5_s, unpad(y_mla_s), p), p)

    st = lambda k: jnp.stack(outs[k], axis=0)
    return (xp.reshape(bp, tp, D_MODEL), xs.reshape(bs, ts, D_MODEL),
            st('c_p'), st('kr_p'), st('c_s'), st('kr_s'),
            st('rw_p'), st('rw_s'), st('sh_p'), st('sh_s'),
            st('gla_p'), st('gla_s'), st('re_p'), st('im_p'), st('re_s'), st('im_s'))
```

```python
import functools

import jax
import jax.numpy as jnp
from jax import lax
from jax.experimental import pallas as pl
from jax.experimental.pallas import tpu as pltpu

F32 = jnp.float32
BF16 = jnp.bfloat16

D_MODEL = 1024
N_BRANCH = 4
BRANCH_W = 256
EPS = 1e-6
PAGE_SIZE = 128
RW_HEADS, RW_HD, RW_W = 4, 64, 256
RW_COLS = 896
RW_GN_EPS = 64e-5
GLA_HEADS, GLA_DK, GLA_DV, GLA_W = 4, 32, 64, 256
GLA_KW = GLA_HEADS * GLA_DK
GLA_A_LORA = 16
GLA_TAU = 16.0
GLA_COLS = 784
GLA_PAD = 896
S5_GROUP, S5_GROUPS, S5_P, S5_W = 16, 16, 64, 256
S5_STATE = S5_GROUPS * S5_P
MLA_HEADS, MLA_NOPE, MLA_ROPE, MLA_QK, MLA_V = 4, 64, 32, 96, 64
MLA_LORA = 256
MLA_COLS = 544
MLA_PAD = 640
MLA_HP = 128
MLA_HW = MLA_HEADS * MLA_HP
MLA_SCALE = MLA_QK ** -0.5
ROPE_THETA = 10000.0
NEG_INF = -1e30
D_FF = 4096
IN_PAD = RW_COLS + GLA_PAD + S5_W + MLA_PAD

V7X_VMEM_BYTES = 64 * 1024 * 1024
VMEM_LIMIT = V7X_VMEM_BYTES - 8 * 1024 * 1024
LANE = 128
SUBLANE = 8

ROW_TILE = 512
RW_CHUNK = 64
RW_BATCH = 4
GLA_SUB = 16
GLA_BATCH = 4
S5_TILE = 128
ATT_TQ = 1024
ATT_TK = 512
DEC_SUB = 512


def _cparams(sem):
    return pltpu.CompilerParams(dimension_semantics=sem, vmem_limit_bytes=VMEM_LIMIT)


def _bf(x):
    return x.astype(BF16)


def _dg(a, b, ca, cb):
    return lax.dot_general(a, b, (((ca,), (cb,)), ((), ())), preferred_element_type=F32)


def _mm(a, b, ca=1, cb=0):
    return _dg(_bf(a), _bf(b), ca, cb)


def _split(x):
    hi = _bf(x)
    return hi, _bf(x - hi.astype(F32))


def _mm3(a, b, ca=1, cb=0):
    ah, al = _split(a)
    bh, bl = _split(b)
    return _dg(ah, bh, ca, cb) + _dg(al, bh, ca, cb) + _dg(ah, bl, ca, cb)


def _mm2(a, b01, ca=1, cb=0):
    ah, al = _split(a)
    return _dg(ah, b01, ca, cb) + _dg(al, b01, ca, cb)


def _mm2l(a01, b, ca=1, cb=0):
    bh, bl = _split(b)
    return _dg(a01, bh, ca, cb) + _dg(a01, bl, ca, cb)


def _sigmoid(x):
    return 1.0 / (1.0 + jnp.exp(-x))


def _softplus(x):
    return jnp.maximum(x, 0.0) + jnp.log(1.0 + jnp.exp(-jnp.abs(x)))


def _iota(shape, dim):
    return lax.broadcasted_iota(jnp.int32, shape, dim)


def _block_ones(n, blk, dtype=BF16):
    return (_iota((n, n), 0) // blk == _iota((n, n), 1) // blk).astype(dtype)


def _full(shape):
    nd = len(shape)
    return pl.BlockSpec(shape, lambda *_: (0,) * nd)


def _norm_proj_kernel(x_ref, g_ref, w_ref, xn_ref, rw_ref, gla_ref, s5_ref, mla_ref):
    x = x_ref[...]
    xn = x * lax.rsqrt(jnp.mean(x * x, axis=-1, keepdims=True) + EPS) * g_ref[...]
    xn_ref[...] = xn
    xb = _bf(xn)
    o = 0
    for ref in (rw_ref, gla_ref, s5_ref, mla_ref):
        w = ref.shape[1]
        ref[...] = jnp.dot(xb, w_ref[:, o:o + w], preferred_element_type=F32)
        o += w


def norm_proj(x, g, w_all):
    n = x.shape[0]
    widths = (RW_COLS, GLA_PAD, S5_W, MLA_PAD)
    row = lambda w: pl.BlockSpec((ROW_TILE, w), lambda i: (i, 0))
    return pl.pallas_call(
        _norm_proj_kernel,
        grid=(n // ROW_TILE,),
        in_specs=[row(D_MODEL), _full((1, D_MODEL)), _full((D_MODEL, IN_PAD))],
        out_specs=[row(D_MODEL)] + [row(w) for w in widths],
        out_shape=[jax.ShapeDtypeStruct((n, D_MODEL), F32)]
        + [jax.ShapeDtypeStruct((n, w), F32) for w in widths],
        compiler_params=_cparams(("parallel",)),
    )(x, g.reshape(1, D_MODEL), w_all)


def _matmul_kernel(a_ref, b_ref, o_ref):
    o_ref[...] = _mm(a_ref[...], b_ref[...])


def small_matmul(a, b):
    m, n = a.shape[0], b.shape[1]
    return pl.pallas_call(
        _matmul_kernel,
        out_shape=jax.ShapeDtypeStruct((m, n), F32),
        compiler_params=_cparams(None),
    )(a, b)


def _rwkv_kernel(has_vres, nb, c, t_valid, *refs):
    if has_vres:
        (prw_ref, sp_ref, s0_ref, vf_ref, mu_ref, w0_ref, w2_ref, a0_ref, a2_ref, g2_ref,
         v0_ref, v1_ref, v2_ref, kk_ref, ka_ref, rk_ref, lng_ref, lnb_ref,
         y_ref, s_ref, carry_ref) = refs
    else:
        (prw_ref, sp_ref, s0_ref, mu_ref, w0_ref, w2_ref, a0_ref, a2_ref, g2_ref,
         kk_ref, ka_ref, rk_ref, lng_ref, lnb_ref,
         y_ref, vout_ref, s_ref, carry_ref) = refs

    @pl.when(pl.program_id(1) == 0)
    def _():
        s_ref[...] = s0_ref[...]
        carry_ref[...] = sp_ref[...]

    n = nb * c
    rowc = _iota((c, 1), 0)
    rown = _iota((n, 1), 0)
    ti, si = _iota((c, c), 0), _iota((c, c), 1)
    tn, sn = _iota((n, n), 0), _iota((n, n), 1)
    tri = ((sn <= tn) & (sn // c == tn // c)).astype(BF16)
    e64 = _block_ones(RW_W, RW_HD)
    rows = lambda fn: jnp.concatenate([fn(bi) for bi in range(nb)], axis=0)

    p = rows(lambda bi: prw_ref[bi])
    prev = rows(lambda bi: jnp.where(rowc == 0, carry_ref[bi], pltpu.roll(prw_ref[bi], shift=1, axis=0)))
    for bi in range(nb):
        carry_ref[bi] = prw_ref[bi, c - 1:c, :]
    rw = p + (prev - p) * mu_ref[...]
    r, k, v = rw[:, 0:256], rw[:, 256:512], rw[:, 512:768]
    tail = rw[:, 768:896]
    wd, ad, gd = tail[:, 0:32], tail[:, 32:64], tail[:, 64:128]
    w_log = -_softplus(-(w0_ref[...] + _mm(jnp.tanh(wd), w2_ref[...]))) - 0.5
    logw = -jnp.exp(w_log)
    a = _sigmoid(a0_ref[...] + _mm(ad, a2_ref[...]))
    g = _mm(_sigmoid(gd), g2_ref[...])
    if has_vres:
        vf = rows(lambda bi: vf_ref[bi])
        v = v + (vf - v) * _sigmoid(v0_ref[...] + _mm(_mm(v, v1_ref[...]), v2_ref[...]))
    else:
        for bi in range(nb):
            vout_ref[bi] = v[bi * c:(bi + 1) * c]
    kk = k * kk_ref[...]
    kk = kk * lax.rsqrt(_mm2(kk * kk, e64) + 1e-12)
    k2 = k * (1.0 + (a - 1.0) * ka_ref[...])
    bv = kk * a
    if t_valid < c:
        valid = rown % c < t_valid
        logw = jnp.where(valid, logw, 0.0)
        kk = jnp.where(valid, kk, 0.0)
        bv = jnp.where(valid, bv, 0.0)
        k2 = jnp.where(valid, k2, 0.0)
        v = jnp.where(valid, v, 0.0)
    g_incl = _mm2l(tri, logw)
    g_last = rows(lambda bi: jnp.broadcast_to(g_incl[(bi + 1) * c - 1:(bi + 1) * c], (c, RW_W)))
    e_neg = jnp.exp(-g_incl)
    e_last = jnp.exp(g_last - g_incl)
    kq = kk * jnp.exp(g_incl - logw)
    rq = r * jnp.exp(g_incl)
    bd, kd = bv * e_neg, k2 * e_neg
    bl, kl = bv * e_last, k2 * e_last
    s_decay = jnp.exp(g_last)

    chains = [(bi, h) for bi in range(nb) for h in range(RW_HEADS)]
    cut = lambda t: [t[bi * c:(bi + 1) * c, h * RW_HD:(h + 1) * RW_HD] for bi, h in chains]
    kq_c, rq_c, bd_c, kd_c, bl_c, kl_c, v_c = cut(kq), cut(rq), cut(bd), cut(kd), cut(bl), cut(kl), cut(v)
    s0_c = [s_ref[bi, h] for bi, h in chains]
    stack = lambda xs, ys: [jnp.concatenate([x, y], axis=0) for x, y in zip(xs, ys)]
    bk_c = stack(bd_c, kd_c)
    kbk = [_mm(x, y, 1, 1) for x, y in zip(kq_c, bk_c)]
    rbk = [_mm(x, y, 1, 1) for x, y in zip(rq_c, bk_c)]
    a_b = [jnp.where(si < ti, x[:, 0:c], 0.0) for x in kbk]
    a_k = [jnp.where(si < ti, x[:, c:2 * c], 0.0) for x in kbk]
    m_b = [jnp.where(si <= ti, x[:, 0:c], 0.0) for x in rbk]
    m_k = [jnp.where(si <= ti, x[:, c:2 * c], 0.0) for x in rbk]
    krs0 = [_mm(x, s, 1, 1) for x, s in zip(stack(kq_c, rq_c), s0_c)]
    amv = [_mm(x, y) for x, y in zip(stack(a_k, m_k), v_c)]
    eye = (si == ti).astype(F32)
    t_inv = [eye - x for x in a_b]
    pw = a_b
    lvl = 2
    while lvl < c:
        pw = [_mm(x, x) for x in pw]
        t_inv = [x + _mm(x, y) for x, y in zip(t_inv, pw)]
        lvl *= 2
    u = [-_mm(x, y[0:c] + z[0:c]) for x, y, z in zip(t_inv, krs0, amv)]
    yh = [y[c:2 * c] + _mm(mb, uu) + z[c:2 * c] for y, mb, uu, z in zip(krs0, m_b, u, amv)]
    upd = [_mm(x, y, 0, 0) for x, y in zip(stack(u, v_c), stack(bl_c, kl_c))]
    for (bi, h), s0, dd in zip(chains, s0_c, upd):
        s_ref[bi, h] = s0 * s_decay[(bi + 1) * c - 1:(bi + 1) * c, h * RW_HD:(h + 1) * RW_HD] + dd

    y = rows(lambda bi: jnp.concatenate(yh[bi * RW_HEADS:(bi + 1) * RW_HEADS], axis=1))
    mean = _mm2(y, e64) * (1.0 / RW_HD)
    yc = y - mean
    var = _mm2(yc * yc, e64) * (1.0 / RW_HD)
    bonus = _mm2(r * k2 * rk_ref[...], e64) * v
    out = (yc * lax.rsqrt(var + RW_GN_EPS) * lng_ref[...] + lnb_ref[...] + bonus) * g
    for bi in range(nb):
        y_ref[bi] = out[bi * c:(bi + 1) * c]


def rwkv_mixer(prw, shift_proj, s0, v_first, p, t_valid):
    b, t, _ = prw.shape
    c = min(RW_CHUNK, t)
    nb = min(RW_BATCH * max(1, RW_CHUNK // (4 * c)), b)
    has_vres = v_first is not None
    tok = lambda w: pl.BlockSpec((nb, c, w), lambda i, j: (i, j, 0))
    per_b = lambda shape: pl.BlockSpec((nb,) + shape, lambda i, j: (i,) + (0,) * len(shape))
    vec = lambda a: a.reshape(1, -1)
    ins = [prw, shift_proj, s0]
    specs = [tok(RW_COLS), per_b((1, RW_COLS)), per_b((RW_HEADS, RW_HD, RW_HD))]
    if has_vres:
        ins.append(v_first)
        specs.append(tok(RW_W))
    params = [vec(p['rw_mu']), vec(p['rw_w0']), p['rw_w2'], vec(p['rw_a0']), p['rw_a2'], p['rw_g2']]
    if has_vres:
        params += [vec(p['rw_v0']), p['rw_v1'], p['rw_v2']]
    params += [vec(p['rw_k_k']), vec(p['rw_k_a']), vec(p['rw_r_k']), vec(p['rw_ln_g']), vec(p['rw_ln_b'])]
    ins += params
    specs += [_full(a.shape) for a in params]
    y_shape = jax.ShapeDtypeStruct((b, t, RW_W), F32)
    s_shape = jax.ShapeDtypeStruct((b, RW_HEADS, RW_HD, RW_HD), F32)
    s_spec = per_b((RW_HEADS, RW_HD, RW_HD))
    if has_vres:
        out_shape, out_specs = [y_shape, s_shape], [tok(RW_W), s_spec]
    else:
        out_shape, out_specs = [y_shape, y_shape, s_shape], [tok(RW_W), tok(RW_W), s_spec]
    outs = pl.pallas_call(
        functools.partial(_rwkv_kernel, has_vres, nb, c, t_valid),
        grid=(b // nb, t // c),
        in_specs=specs,
        out_specs=out_specs,
        out_shape=out_shape,
        scratch_shapes=[pltpu.VMEM((nb, 1, RW_COLS), F32)],
        compiler_params=_cparams(("parallel", "arbitrary")),
    )(*ins)
    if has_vres:
        return outs[0], v_first, outs[1]
    return outs[0], outs[1], outs[2]


def _gla_kernel(nb, c, sub, t_valid, pg_ref, s0_ref, a2_ref, ab_ref, ng_ref, y_ref, s_ref):
    @pl.when(pl.program_id(1) == 0)
    def _():
        s_ref[...] = s0_ref[...]

    expand_b = (_iota((GLA_KW, GLA_W), 0) // GLA_DK == _iota((GLA_KW, GLA_W), 1) // GLA_DV).astype(BF16)
    own_block = (_iota((GLA_W, GLA_KW), 0) // GLA_DV == _iota((GLA_W, GLA_KW), 1) // GLA_DK)
    e64 = _block_ones(GLA_W, GLA_DV)
    n = nb * c
    nsub = c // sub
    tn, sn = _iota((n, n), 0), _iota((n, n), 1)
    tri = ((sn <= tn) & (sn // sub == tn // sub)).astype(BF16)
    row = _iota((sub, 1), 0)
    rown = _iota((n, 1), 0)
    pg = jnp.concatenate([pg_ref[bi] for bi in range(nb)], axis=0)
    q = pg[:, 0:128] * (GLA_DK ** -0.5)
    k = pg[:, 128:256]
    v = pg[:, 256:512]
    rg = pg[:, 512:768]
    z = _mm(pg[:, 768:768 + GLA_A_LORA], a2_ref[...]) + ab_ref[...]
    la = (jnp.minimum(z, 0.0) - jnp.log(1.0 + jnp.exp(-jnp.abs(z)))) / GLA_TAU
    if t_valid < c:
        valid = rown % c < t_valid
        la = jnp.where(valid, la, 0.0)
        k = jnp.where(valid, k, 0.0)
    bcum = _mm2l(tri, la)
    qdec = q * jnp.exp(bcum)

    blocks = [(bi, j) for bi in range(nb) for j in range(nsub)]
    cut = lambda t: [t[bi * c + j * sub:bi * c + (j + 1) * sub] for bi, j in blocks]
    q_b, k_b, v_b, b_b, qd_b = cut(q), cut(k), cut(v), cut(bcum), cut(qdec)
    amat = []
    for qq, kk, bb in zip(q_b, k_b, b_b):
        prods = [jnp.where(row >= s, qq * kk[s:s + 1] * jnp.exp(jnp.minimum(bb - bb[s:s + 1], 0.0)), 0.0)
                 for s in range(sub)]
        amat.append(_mm(jnp.concatenate(prods, axis=0), expand_b))
    upd = [jnp.where(own_block, _mm(vv, kk * jnp.exp(bb[sub - 1:sub] - bb), 0, 0), 0.0)
           for vv, kk, bb in zip(v_b, k_b, b_b)]
    states = []
    for bi in range(nb):
        st = s_ref[bi]
        for j in range(nsub):
            states.append(st)
            i = bi * nsub + j
            st = st * jnp.exp(b_b[i][sub - 1:sub]) + upd[i]
        s_ref[bi] = st
    outs = []
    for i in range(len(blocks)):
        o = _mm(qd_b[i], states[i], 1, 1)
        for s in range(sub):
            o = o + amat[i][s * sub:(s + 1) * sub] * v_b[i][s:s + 1]
        outs.append(o)
    o = jnp.concatenate(outs, axis=0)
    ms = _mm2(o * o, e64) * (1.0 / GLA_DV)
    y = o * lax.rsqrt(ms + EPS) * ng_ref[...] * (rg * _sigmoid(rg))
    for bi in range(nb):
        y_ref[bi] = y[bi * c:(bi + 1) * c]


def gla_mixer(pg, s0t, p, t_valid):
    b, t, _ = pg.shape
    sub = min(GLA_SUB, t)
    c = min(4 * sub, t)
    nb = min(GLA_BATCH, b)
    tok = lambda w: pl.BlockSpec((nb, c, w), lambda i, j: (i, j, 0))
    st_spec = pl.BlockSpec((nb, GLA_W, GLA_KW), lambda i, j: (i, 0, 0))
    ng = jnp.tile(p['gla_norm_g'], GLA_HEADS).reshape(1, GLA_W)
    return pl.pallas_call(
        functools.partial(_gla_kernel, nb, c, sub, t_valid),
        grid=(b // nb, t // c),
        in_specs=[tok(GLA_PAD), st_spec, _full((GLA_A_LORA, GLA_KW)), _full((1, GLA_KW)), _full((1, GLA_W))],
        out_specs=[tok(GLA_W), st_spec],
        out_shape=[jax.ShapeDtypeStruct((b, t, GLA_W), F32), jax.ShapeDtypeStruct((b, GLA_W, GLA_KW), F32)],
        compiler_params=_cparams(("parallel", "arbitrary")),
    )(pg, s0t, p['gla_a2'], p['gla_ab'].reshape(1, GLA_KW), ng)


def _s5_kernel(rows, seg, carry_state, u_ref, hre_ref, him_ref, pre_ref, pim_ref, bre_ref, bim_ref,
               cre_ref, cim_ref, d_ref, ga_ref, gb_ref, y_ref, ore_ref, oim_ref, *scratch):
    u = u_ref[...]
    drive = _mm if carry_state else _mm3
    h_re = drive(u, bre_ref[...])
    h_im = drive(u, bim_ref[...])
    tpos = _iota((rows, 1), 0) % seg
    d = 1
    while d < seg:
        ar, ai = pre_ref[d - 1:d], pim_ref[d - 1:d]
        sr = jnp.where(tpos >= d, pltpu.roll(h_re, shift=d, axis=0), 0.0)
        sm = jnp.where(tpos >= d, pltpu.roll(h_im, shift=d, axis=0), 0.0)
        h_re, h_im = h_re + ar * sr - ai * sm, h_im + ar * sm + ai * sr
        d *= 2
    if carry_state:
        cre_s, cim_s = scratch

        @pl.when(pl.program_id(1) == 0)
        def _():
            cre_s[...] = hre_ref[0]
            cim_s[...] = him_ref[0]
        i_re, i_im = cre_s[...], cim_s[...]
        pr, pi = pre_ref[0:seg], pim_ref[0:seg]
        parts_re, parts_im = [], []
        for k in range(rows // seg):
            sl = slice(k * seg, (k + 1) * seg)
            p_re = h_re[sl] + pr * i_re - pi * i_im
            p_im = h_im[sl] + pr * i_im + pi * i_re
            i_re, i_im = p_re[seg - 1:seg], p_im[seg - 1:seg]
            parts_re.append(p_re)
            parts_im.append(p_im)
        h_re, h_im = jnp.concatenate(parts_re, axis=0), jnp.concatenate(parts_im, axis=0)
        cre_s[...] = i_re
        cim_s[...] = i_im
        ore_ref[0] = i_re
        oim_ref[0] = i_im
    else:
        i_re, i_im = hre_ref[...], him_ref[...]
        pr, pi = pre_ref[...], pim_ref[...]
        h_re, h_im = h_re + pr * i_re - pi * i_im, h_im + pr * i_im + pi * i_re
        ore_ref[...] = h_re
        oim_ref[...] = h_im
    y = _mm(h_re, cre_ref[...]) - _mm(h_im, cim_ref[...]) + d_ref[...] * u
    z = 0.5 * y * (1.0 + jnp.tanh(0.7978845608028654 * (y + 0.044715 * y * y * y)))
    y_ref[...] = _mm(z, ga_ref[...]) * _sigmoid(_mm(z, gb_ref[...]))


def s5_mixer(u, h0_re, h0_im, p, seq_len):
    n = u.shape[0]
    b = n // seq_len
    f32 = F32
    a_re, a_im = p['s5_a_re'], p['s5_a_im']
    dt = jnp.exp(p['s5_log_dt'])[:, None]
    mag = jnp.exp(a_re * dt)
    ab_re, ab_im = mag * jnp.cos(a_im * dt), mag * jnp.sin(a_im * dt)
    den = a_re * a_re + a_im * a_im
    cr = ((ab_re - 1.0) * a_re + ab_im * a_im) / den
    ci = (ab_im * a_re - (ab_re - 1.0) * a_im) / den
    bb_re = cr[..., None] * p['s5_b_re'] - ci[..., None] * p['s5_b_im']
    bb_im = cr[..., None] * p['s5_b_im'] + ci[..., None] * p['s5_b_re']
    eye_g = jnp.eye(S5_GROUPS, dtype=f32)

    def in_blockdiag(w):
        return jnp.einsum('gpi,gh->gihp', w, eye_g).reshape(S5_W, S5_STATE)

    def out_blockdiag(w):
        return jnp.einsum('gip,gh->gphi', w, eye_g).reshape(S5_STATE, S5_W)

    carry_state = seq_len > S5_TILE
    rows = S5_TILE
    seg = SUBLANE if carry_state else seq_len
    pw_re, pw_im = ab_re.reshape(1, S5_STATE), ab_im.reshape(1, S5_STATE)
    while pw_re.shape[0] < seg:
        tr, ti_ = pw_re[-1:], pw_im[-1:]
        pw_re, pw_im = (jnp.concatenate([pw_re, pw_re * tr - pw_im * ti_], axis=0),
                        jnp.concatenate([pw_im, pw_re * ti_ + pw_im * tr], axis=0))
    pw_re, pw_im = jnp.tile(pw_re, (rows // seg, 1)), jnp.tile(pw_im, (rows // seg, 1))
    weights = [pw_re, pw_im, in_blockdiag(bb_re), in_blockdiag(bb_im),
               out_blockdiag(p['s5_c_re']), out_blockdiag(p['s5_c_im']),
               p['s5_d'].reshape(1, S5_W), p['s5_glu_a'], p['s5_glu_b']]
    wspecs = [_full(w.shape) for w in weights]
    if carry_state:
        tiles = seq_len // rows
        grid = (b, tiles)
        tok = lambda w: pl.BlockSpec((rows, w), lambda i, j: (i * tiles + j, 0))
        st_in = pl.BlockSpec((1, 1, S5_STATE), lambda i, j: (i, 0, 0))
        ins = [u, h0_re.reshape(b, 1, S5_STATE), h0_im.reshape(b, 1, S5_STATE)]
        st_shape = jax.ShapeDtypeStruct((b, 1, S5_STATE), f32)
        scratch = [pltpu.VMEM((1, S5_STATE), f32)] * 2
        sem = ("parallel", "arbitrary")
        st_out = st_in
    else:
        grid = (n // rows,)
        tok = lambda w: pl.BlockSpec((rows, w), lambda i: (i, 0))
        st_in = tok(S5_STATE)
        ins = [u, jnp.repeat(h0_re, seq_len, axis=0), jnp.repeat(h0_im, seq_len, axis=0)]
        st_shape = jax.ShapeDtypeStruct((n, S5_STATE), f32)
        scratch = []
        sem = ("parallel",)
        st_out = st_in
    y, o_re, o_im = pl.pallas_call(
        functools.partial(_s5_kernel, rows, seg, carry_state),
        grid=grid,
        in_specs=[tok(S5_W), st_in, st_in] + wspecs,
        out_specs=[tok(S5_W), st_out, st_out],
        out_shape=[jax.ShapeDtypeStruct((n, S5_W), f32), st_shape, st_shape],
        scratch_shapes=scratch,
        compiler_params=_cparams(sem),
    )(*ins, *weights)
    if carry_state:
        return y, o_re.reshape(b, S5_STATE), o_im.reshape(b, S5_STATE)
    last = lambda h: h.reshape(b, seq_len, S5_STATE)[:, -1]
    return y, last(o_re), last(o_im)


def _rope_rotate(x, cos_t, sin_a, sin_b):
    w = x.shape[1]
    return x * cos_t + pltpu.roll(x, shift=w - MLA_ROPE // 2, axis=1) * sin_a + pltpu.roll(x, shift=MLA_ROPE // 2, axis=1) * sin_b


def _mla_keys(c, kr, cos_t, sin_a, sin_b, wuk_pad, place):
    k_nope = _mm(c, wuk_pad)
    k_rope = _rope_rotate(_mm3(kr, place), cos_t, sin_a, sin_b)
    return k_nope, k_rope


def _mla_prep_kernel(pm_ref, cos_ref, sina_ref, sinb_ref, qag_ref, wuq_ref, qg_ref, kvag_ref,
                     wuk_ref, wuv_ref, pk_ref, q_ref, k_ref, v_ref, c_ref, kr_ref):
    pm = pm_ref[...]
    qd, kvd, kr = pm[:, 0:256], pm[:, 256:512], pm[:, 512:512 + MLA_ROPE]
    tile4 = lambda t: jnp.concatenate([t] * MLA_HEADS, axis=1)
    cos_t, sin_a, sin_b = tile4(cos_ref[...]), tile4(sina_ref[...]), tile4(sinb_ref[...])
    e128 = _block_ones(MLA_HW, MLA_HP)
    qn = qd * lax.rsqrt(jnp.mean(qd * qd, axis=-1, keepdims=True) + EPS) * qag_ref[...]
    q = _mm(qn, wuq_ref[...])
    q = q * lax.rsqrt(_mm2(q * q, e128) * (1.0 / MLA_QK) + EPS) * qg_ref[...]
    q = _rope_rotate(q, cos_t, sin_a, sin_b)
    c = kvd * lax.rsqrt(jnp.mean(kvd * kvd, axis=-1, keepdims=True) + EPS) * kvag_ref[...]
    c_ref[...] = c
    kr_ref[...] = kr
    k_nope, k_rope = _mla_keys(c, kr, cos_t, sin_a, sin_b, wuk_ref[...], pk_ref[...])
    ssq = _mm2(k_nope * k_nope, e128) + jnp.sum(kr * kr, axis=-1, keepdims=True)
    kfull = (k_nope + k_rope) * (lax.rsqrt(ssq * (1.0 / MLA_QK) + EPS) * MLA_SCALE)
    vfull = _mm(c, wuv_ref[...])
    for h in range(MLA_HEADS):
        hs = slice(h * MLA_HP, (h + 1) * MLA_HP)
        q_ref[h] = _bf(q[:, hs])
        k_ref[h] = _bf(kfull[:, hs])
        v_ref[h] = _bf(vfull[:, hs])


def _flash_kernel(tq, tk, q_ref, k_ref, v_ref, o_ref):
    iq = pl.program_id(2)
    q = q_ref[0]
    ratio = tq // tk

    def step(ik, carry, masked):
        m, l, acc = carry
        off = pl.multiple_of(ik * tk, tk)
        s = _dg(q, k_ref[0, pl.ds(off, tk), :], 1, 1)
        if masked:
            s = jnp.where(off + _iota((tq, tk), 1) <= iq * tq + _iota((tq, tk), 0), s, NEG_INF)
        m_new = jnp.maximum(m, jnp.max(s, axis=-1, keepdims=True))
        alpha = jnp.exp(m - m_new)
        pexp = jnp.exp(s - m_new)
        l = alpha * l + jnp.sum(pexp, axis=-1, keepdims=True)
        acc = alpha * acc + jnp.dot(_bf(pexp), v_ref[0, pl.ds(off, tk), :], preferred_element_type=F32)
        return m_new, l, acc

    carry = (jnp.full((tq, 1), NEG_INF, F32), jnp.zeros((tq, 1), F32), jnp.zeros((tq, MLA_HP), F32))
    carry = lax.fori_loop(0, iq * ratio, lambda ik, cr: step(ik, cr, False), carry)
    for d in range(ratio):
        carry = step(iq * ratio + d, carry, True)
    m, l, acc = carry
    o_ref[...] = acc / l


def mla_prompt(pm, p, batch, seq_len, tables):
    n = pm.shape[0]
    rows = min(ROW_TILE, seq_len)
    tiles = seq_len // rows
    cos_t, sin_a, sin_b = tables
    tok = lambda w: pl.BlockSpec((rows, w), lambda i: (i, 0))
    tab = pl.BlockSpec((rows, MLA_HP), lambda i: (i % tiles, 0))
    hd = pl.BlockSpec((MLA_HEADS, rows, MLA_HP), lambda i: (0, i, 0))
    weights = [p['qag'], p['wuq_pad'], p['qg_pad'], p['kvag'], p['wuk_pad'], p['wuv_pad'], p['pk']]
    hshape = jax.ShapeDtypeStruct((MLA_HEADS, n, MLA_HP), BF16)
    q, k, v, c, kr = pl.pallas_call(
        _mla_prep_kernel,
        grid=(n // rows,),
        in_specs=[tok(MLA_PAD), tab, tab, tab] + [_full(w.shape) for w in weights],
        out_specs=[hd, hd, hd, tok(MLA_LORA), tok(MLA_ROPE)],
        out_shape=[hshape, hshape, hshape, jax.ShapeDtypeStruct((n, MLA_LORA), F32),
                   jax.ShapeDtypeStruct((n, MLA_ROPE), F32)],
        compiler_params=_cparams(("parallel",)),
    )(pm, cos_t, sin_a, sin_b, *weights)
    tq = min(ATT_TQ, seq_len)
    tk = min(ATT_TK, tq)
    nq = seq_len // tq
    kv_spec = pl.BlockSpec((1, seq_len, MLA_HP), lambda b, h, i: (h, b, 0))
    o = pl.pallas_call(
        functools.partial(_flash_kernel, tq, tk),
        grid=(batch, MLA_HEADS, nq),
        in_specs=[pl.BlockSpec((1, tq, MLA_HP), lambda b, h, i: (h, b * nq + i, 0)), kv_spec, kv_spec],
        out_specs=pl.BlockSpec((tq, MLA_HP), lambda b, h, i: (b * nq + i, h)),
        out_shape=jax.ShapeDtypeStruct((n, MLA_HW), F32),
        compiler_params=_cparams(("parallel", "parallel", "arbitrary")),
    )(q, k, v)
    return o, c, kr


def _mla_decode_kernel(n_pages, layer, t_new, *refs):
    (pt_ref, pm_ref, tabn_ref, tabp_ref, vec_ref, w3_ref, pkg_ref, wuk_ref, cache_c, cache_kr,
     o_ref, c_out, kr_out, cbf_s, cbuf, kbuf, sem) = refs
    step = pl.program_id(0)
    slot = step % 2

    def page_copies(seq, buf, g):
        page = pt_ref[seq, g]
        return (pltpu.make_async_copy(cache_c.at[layer, page], cbuf.at[buf, g], sem.at[0, buf]),
                pltpu.make_async_copy(cache_kr.at[layer, page], kbuf.at[buf, g], sem.at[1, buf]))

    def request(seq, buf):
        for g in range(n_pages):
            for cp in page_copies(seq, buf, g):
                cp.start()

    @pl.when(step == 0)
    def _():
        request(0, 0)

    @pl.when(step + 1 < pl.num_programs(0))
    def _():
        request(step + 1, 1 - slot)

    for g in range(n_pages):
        for cp in page_copies(step, slot, g):
            cp.wait()
    c_refs = [cbuf.at[slot, g] for g in range(n_pages)]
    kr_refs = [kbuf.at[slot, g] for g in range(n_pages)]
    qag, kvag, qg = vec_ref[0:1, 0:MLA_LORA], vec_ref[1:2, 0:MLA_LORA], vec_ref[2:3, :]
    place, kgain = pkg_ref[:, 0:MLA_HW], pkg_ref[:, MLA_HW:MLA_HW + 1]
    nq = MLA_HEADS * SUBLANE
    n = n_pages * PAGE_SIZE
    half = MLA_ROPE // 2
    tile4 = lambda t: jnp.concatenate([t] * MLA_HEADS, axis=1)
    row_head = (_iota((nq, MLA_HW), 0) // SUBLANE == _iota((nq, MLA_HW), 1) // MLA_HP)
    head_sel = (_iota((nq, MLA_LORA), 0) // SUBLANE == _iota((nq, MLA_LORA), 1) // MLA_NOPE)

    pm = pm_ref[0]
    qd, kvd, kr_new = pm[:, 0:256], pm[:, 256:512], pm[:, 512:512 + MLA_ROPE]
    c_new = kvd * lax.rsqrt(jnp.mean(kvd * kvd, axis=-1, keepdims=True) + EPS) * kvag
    c_out[0] = c_new
    kr_out[0] = kr_new
    cos_n, sin_an, sin_bn = (tile4(tabn_ref[:, 0:MLA_HP]), tile4(tabn_ref[:, MLA_HP:2 * MLA_HP]),
                             tile4(tabn_ref[:, 2 * MLA_HP:3 * MLA_HP]))

    for g in range(n_pages):
        cbf_s[g * PAGE_SIZE:(g + 1) * PAGE_SIZE, :] = _bf(c_refs[g][...])
    kt = jnp.concatenate([kr_refs[g][...] for g in range(n_pages)], axis=1)
    sq_rope = jnp.sum(kt * kt, axis=0, keepdims=True)
    kg = kt * kgain
    x1, x2 = kg[0:half], kg[half:]
    cos_p, sin_p = tabp_ref[0:half, :], tabp_ref[half:MLA_ROPE, :]
    krot = _bf(jnp.concatenate([x1 * cos_p - x2 * sin_p, x2 * cos_p + x1 * sin_p], axis=0))

    env = {}

    def q_project():
        qn = qd * lax.rsqrt(jnp.mean(qd * qd, axis=-1, keepdims=True) + EPS) * qag
        env['q'] = _mm(qn, w3_ref[:, 0:MLA_HW])

    def q_norm():
        q = env['q']
        env['q'] = q * lax.rsqrt(_mm2(q * q, _block_ones(MLA_HW, MLA_HP)) * (1.0 / MLA_QK) + EPS) * qg

    def q_rotate():
        q = _rope_rotate(env['q'], cos_n, sin_an, sin_bn)
        env['q4'] = jnp.concatenate([q] * MLA_HEADS, axis=0)
        env['q_stack'] = jnp.where(row_head, env['q4'], 0.0)

    def q_compact():
        li, lo = _iota((MLA_HW, MLA_LORA), 0), _iota((MLA_HW, MLA_LORA), 1)
        take_nope = ((li // MLA_HP == lo // MLA_NOPE) & (li % MLA_HP == lo % MLA_NOPE)).astype(BF16)
        env['q_nope'] = _bf(jnp.where(head_sel, _mm(env['q4'], take_nope), 0.0))
        ri, ro = _iota((MLA_HW, MLA_ROPE), 0), _iota((MLA_HW, MLA_ROPE), 1)
        take_rope = (ri % MLA_HP == MLA_NOPE + ro).astype(BF16)
        env['q_rope'] = _bf(_mm(env['q_stack'], take_rope))

    def rope_scores():
        env['s_rope'] = jnp.dot(env['q_rope'], krot, preferred_element_type=F32)

    def new_keys():
        k_nope, k_rope = _mla_keys(c_new, kr_new, cos_n, sin_an, sin_bn, w3_ref[:, MLA_HW:2 * MLA_HW], place)
        ssq_n = _mm2l(row_head.astype(BF16), k_nope * k_nope, 1, 1)
        ssq_n = ssq_n + _mm2l(jnp.ones((SUBLANE, MLA_ROPE), BF16), kr_new * kr_new, 1, 1)[0:1]
        env['k_new'] = k_nope + k_rope
        env['norm_new'] = lax.rsqrt(ssq_n * (1.0 / MLA_QK) + EPS) * MLA_SCALE

    def new_block():
        s_n = _mm(env['q_stack'], env['k_new'], 1, 1) * env['norm_new']
        qry_t = _iota((nq, SUBLANE), 0) % SUBLANE
        key_t = _iota((nq, SUBLANE), 1)
        s_n = jnp.where((key_t <= qry_t) & (key_t < t_new), s_n, NEG_INF)
        m_n = jnp.max(s_n, axis=-1, keepdims=True)
        p_n = jnp.exp(s_n - m_n)
        env['new'] = dict(m=m_n, l=jnp.sum(p_n, axis=-1, keepdims=True), acc=_mm(p_n, c_new))

    chain = [q_project, new_keys, q_norm, q_rotate, q_compact, rope_scores, new_block]
    hold = len(chain)

    nt = n // DEC_SUB
    sel_b = _bf(head_sel.astype(F32))
    lanes = lambda t, i: t[:, i * DEC_SUB:(i + 1) * DEC_SUB]
    val = [dict() for _ in range(nt)]

    def stage(k, i):
        d = val[i]
        if k == 0:
            d['cb'] = cbf_s[i * DEC_SUB:(i + 1) * DEC_SUB, :]
            d['kpre'] = jnp.dot(d['cb'], wuk_ref[...], preferred_element_type=F32)
        elif k == 1:
            d['ksq'], d['kb'] = _bf(d['kpre'] * d['kpre']), _bf(d['kpre'])
        elif k == 2:
            d['ssq'] = _dg(sel_b, d['ksq'], 1, 1)
            d['sc'] = _dg(env['q_nope'], d['kb'], 1, 1)
        elif k == 3:
            norm = lax.rsqrt((d['ssq'] + lanes(sq_rope, i)) * (1.0 / MLA_QK) + EPS) * MLA_SCALE
            d['s'] = (d['sc'] + lanes(env['s_rope'], i)) * norm
            d['m'] = jnp.max(d['s'], axis=-1, keepdims=True)
        elif k == 4:
            pexp = jnp.exp(d['s'] - d['m'])
            d['l'] = jnp.sum(pexp, axis=-1, keepdims=True)
            d['acc'] = jnp.dot(_bf(pexp), d['cb'], preferred_element_type=F32)

    n_stage = 5
    issue_time = lambda k, i: i + k + (hold if k >= 2 else 0)
    for t in range(issue_time(n_stage - 1, nt - 1) + 1):
        if t < len(chain):
            chain[t]()
        for k in range(n_stage):
            for i in range(nt):
                if issue_time(k, i) == t:
                    stage(k, i)
    blocks = [env['new']] + val
    m_all = blocks[0]['m']
    for d in blocks[1:]:
        m_all = jnp.maximum(m_all, d['m'])
    l_all, acc = None, None
    for d in blocks:
        w = jnp.exp(d['m'] - m_all)
        l_all = w * d['l'] if l_all is None else l_all + w * d['l']
        acc = w * d['acc'] if acc is None else acc + w * d['acc']
    o_lat = acc / l_all
    full = jnp.where(row_head, _mm(o_lat, w3_ref[:, 2 * MLA_HW:3 * MLA_HW]), 0.0)
    gather = (_iota((SUBLANE, nq), 0) == _iota((SUBLANE, nq), 1) % SUBLANE).astype(BF16)
    o_ref[0] = _mm2l(gather, full)


def mla_decode(pm, cache_c, cache_krt, page_table, layer, p, tables_new, tables_past_t, t_new):
    b = pm.shape[0]
    pages = page_table.shape[1]
    n = pages * PAGE_SIZE
    per_b = lambda w: pl.BlockSpec((1, SUBLANE, w), lambda i, pt: (i, 0, 0))
    tab_new = pl.BlockSpec((SUBLANE, 3 * MLA_HP), lambda i, pt: (0, 0))
    tab_past = pl.BlockSpec((MLA_ROPE, n), lambda i, pt: (0, 0))
    pad_row = lambda v: jnp.pad(v, ((0, 0), (0, MLA_HW - v.shape[1])))
    vecs = jnp.pad(jnp.concatenate([pad_row(p['qag']), pad_row(p['kvag']), p['qg_pad']], axis=0),
                   ((0, SUBLANE - 3), (0, 0)))
    w3 = jnp.concatenate([p['wuq_pad'], p['wuk_pad'], p['wuv_pad']], axis=1)
    pkg = jnp.concatenate([p['pk'], p['kgain_col'], jnp.zeros((MLA_ROPE, LANE - 1), F32)], axis=1)
    weights = [vecs, w3, pkg, p['wuk']]
    wspecs = [pl.BlockSpec(w.shape, lambda i, pt, nd=w.ndim: (0,) * nd) for w in weights]

    hbm = pl.BlockSpec(memory_space=pl.ANY)
    grid_spec = pltpu.PrefetchScalarGridSpec(
        num_scalar_prefetch=1,
        grid=(b,),
        in_specs=[per_b(MLA_PAD), tab_new, tab_past] + wspecs + [hbm, hbm],
        out_specs=[per_b(MLA_HW), per_b(MLA_LORA), per_b(MLA_ROPE)],
        scratch_shapes=[pltpu.VMEM((n, MLA_LORA), BF16),
                        pltpu.VMEM((2, pages, PAGE_SIZE, MLA_LORA), F32),
                        pltpu.VMEM((2, pages, MLA_ROPE, PAGE_SIZE), F32),
                        pltpu.SemaphoreType.DMA((2, 2))],
    )
    o, c, kr = pl.pallas_call(
        functools.partial(_mla_decode_kernel, pages, layer, t_new),
        grid_spec=grid_spec,
        out_shape=[jax.ShapeDtypeStruct((b, SUBLANE, MLA_HW), F32),
                   jax.ShapeDtypeStruct((b, SUBLANE, MLA_LORA), F32),
                   jax.ShapeDtypeStruct((b, SUBLANE, MLA_ROPE), F32)],
        compiler_params=_cparams(("arbitrary",)),
    )(page_table, pm, jnp.concatenate(tables_new, axis=1), jnp.concatenate(tables_past_t, axis=0), *weights,
      cache_c, cache_krt)
    return o, c, kr


def rope_tables_transposed(pos):
    half = MLA_ROPE // 2
    inv = ROPE_THETA ** (-jnp.arange(half, dtype=F32) / half)
    ang = inv[:, None] * pos.astype(F32)[None, :]
    return jnp.cos(ang), jnp.sin(ang)


def rope_tables(pos):
    half = MLA_ROPE // 2
    inv = ROPE_THETA ** (-jnp.arange(half, dtype=F32) / half)
    ang = pos.astype(F32)[:, None] * inv
    cos, sin = jnp.cos(ang), jnp.sin(ang)
    n = pos.shape[0]
    ones, zeros = jnp.ones((n, MLA_NOPE), F32), jnp.zeros((n, MLA_NOPE), F32)
    zpad = jnp.zeros((n, MLA_HP - MLA_QK), F32)
    zh = jnp.zeros((n, half), F32)
    cos_t = jnp.concatenate([ones, cos, cos, zpad + 1.0], axis=1)
    sin_a = jnp.concatenate([zeros, -sin, zh, zpad], axis=1)
    sin_b = jnp.concatenate([zeros, zh, sin, zpad], axis=1)
    return cos_t, sin_a, sin_b


def _merge_kernel(x_ref, xn_ref, yrw_ref, ygla_ref, ys5_ref, ymla_ref, wg_ref, wb_ref, wbm_ref, wo_ref, o_ref):
    xb = _bf(xn_ref[...])
    merged = None
    for n, y_ref in enumerate((yrw_ref, ygla_ref, ys5_ref, ymla_ref)):
        gate = _sigmoid(jnp.dot(xb, wg_ref[:, n * D_MODEL:(n + 1) * D_MODEL], preferred_element_type=F32))
        w = wbm_ref[...] if n == 3 else wb_ref[n]
        term = gate * jnp.dot(_bf(y_ref[...]), w, preferred_element_type=F32)
        merged = term if merged is None else merged + term
    o_ref[...] = x_ref[...] + jnp.dot(_bf(merged), wo_ref[...], preferred_element_type=F32)


def merge(x, xn, y_rw, y_gla, y_s5, y_mla, p):
    n = x.shape[0]
    row = lambda w: pl.BlockSpec((ROW_TILE, w), lambda i: (i, 0))
    weights = [p['w_gate'], p['w_branch3'], p['w_branch_mla'], p['w_out']]
    return pl.pallas_call(
        _merge_kernel,
        grid=(n // ROW_TILE,),
        in_specs=[row(D_MODEL), row(D_MODEL), row(BRANCH_W), row(BRANCH_W), row(BRANCH_W), row(MLA_HW)]
        + [_full(w.shape) for w in weights],
        out_specs=row(D_MODEL),
        out_shape=jax.ShapeDtypeStruct((n, D_MODEL), F32),
        compiler_params=_cparams(("parallel",)),
    )(x, xn, y_rw, y_gla, y_s5, y_mla, *weights)


def _mlp_kernel(ff_chunk, x_ref, g_ref, w1_ref, w2_ref, o_ref):
    x = x_ref[...]
    xb = _bf(x * lax.rsqrt(jnp.mean(x * x, axis=-1, keepdims=True) + EPS) * g_ref[...])
    acc = x
    for o in range(0, D_FF, ff_chunk):
        h = jnp.maximum(jnp.dot(xb, w1_ref[:, o:o + ff_chunk], preferred_element_type=F32), 0.0)
        acc = acc + jnp.dot(_bf(h * h), w2_ref[o:o + ff_chunk, :], preferred_element_type=F32)
    o_ref[...] = acc


def mlp(x, p):
    n = x.shape[0]
    row = pl.BlockSpec((ROW_TILE, D_MODEL), lambda i: (i, 0))
    return pl.pallas_call(
        functools.partial(_mlp_kernel, 1024),
        grid=(n // ROW_TILE,),
        in_specs=[row, _full((1, D_MODEL)), _full((D_MODEL, D_FF)), _full((D_FF, D_MODEL))],
        out_specs=row,
        out_shape=jax.ShapeDtypeStruct((n, D_MODEL), F32),
        compiler_params=_cparams(("parallel",)),
    )(x, p['norm2_g'].reshape(1, D_MODEL), p['w_ff1'], p['w_ff2'])


def _pad_cols(w, width):
    return jnp.pad(w, ((0, 0), (0, width - w.shape[1])))


def _head_pad(w, used):
    kdim = w.shape[0]
    w = w.reshape(kdim, MLA_HEADS, used)
    return jnp.pad(w, ((0, 0), (0, 0), (0, MLA_HP - used))).reshape(kdim, MLA_HW)


def prepare_layer(l, W):
    p = {}
    w_in = W['w_in'][l]
    o_gla, o_s5, o_mla = RW_COLS, RW_COLS + GLA_COLS, RW_COLS + GLA_COLS + S5_W
    w_gla = w_in[:, o_gla:o_s5]
    w_gla = jnp.concatenate([w_gla[:, 0:512], w_gla[:, 528:784], w_gla[:, 512:528]], axis=1)
    p['w_rw'] = _bf(w_in[:, :RW_COLS])
    p['w_all'] = _bf(jnp.concatenate([w_in[:, :RW_COLS], _pad_cols(w_gla, GLA_PAD), w_in[:, o_s5:o_mla],
                                      _pad_cols(w_in[:, o_mla:], MLA_PAD)], axis=1))
    for name in ('norm1_g', 'rw_mu', 'rw_w0', 'rw_w2', 'rw_a0', 'rw_a2', 'rw_g2', 'rw_k_k', 'rw_k_a',
                 'rw_ln_g', 'rw_ln_b', 'gla_a2', 'gla_ab', 'gla_norm_g', 's5_a_re', 's5_a_im', 's5_log_dt',
                 's5_b_re', 's5_b_im', 's5_c_re', 's5_c_im', 's5_glu_a', 's5_glu_b', 'norm2_g'):
        p[name] = W[name][l]
    p['rw_r_k'] = W['rw_r_k'][l].reshape(RW_W)
    p['s5_d'] = W['s5_d'][l].reshape(S5_W)
    if l > 0:
        for name in ('rw_v0', 'rw_v1', 'rw_v2'):
            p[name] = W[name][l - 1]
    k_g = W['mla_k_g'][l]
    p['qag'] = W['mla_qa_g'][l].reshape(1, MLA_LORA)
    p['kvag'] = W['mla_kva_g'][l].reshape(1, MLA_LORA)
    p['wuq_pad'] = _head_pad(W['mla_w_uq'][l], MLA_QK)
    qg = W['mla_q_g'][l] * jnp.concatenate([k_g[:MLA_NOPE], jnp.ones((MLA_ROPE,), F32)])
    p['qg_pad'] = jnp.tile(jnp.pad(qg, (0, MLA_HP - MLA_QK)), MLA_HEADS).reshape(1, MLA_HW)
    p['wuk_pad'] = _head_pad(W['mla_w_uk'][l].reshape(MLA_LORA, MLA_HEADS * MLA_NOPE), MLA_NOPE)
    wuv = W['mla_w_uv'][l].reshape(MLA_LORA, MLA_HEADS * MLA_V)
    p['wuv_pad'] = _head_pad(wuv, MLA_V)
    p['wuk'] = _bf(W['mla_w_uk'][l].reshape(MLA_LORA, MLA_HEADS * MLA_NOPE))
    p['kgain_col'] = k_g[MLA_NOPE:].reshape(MLA_ROPE, 1)
    place = jnp.pad(jnp.diag(k_g[MLA_NOPE:]), ((0, 0), (MLA_NOPE, MLA_HP - MLA_QK)))
    p['pk'] = jnp.tile(place, (1, MLA_HEADS))
    p['w_gate'] = _bf(W['w_gate'][l])
    wb = W['w_branch'][l]
    p['w_branch3'] = _bf(wb)
    wbm = wb[3].reshape(MLA_HEADS, MLA_V, D_MODEL)
    p['w_branch_mla'] = _bf(jnp.pad(wbm, ((0, 0), (0, MLA_HP - MLA_V), (0, 0))).reshape(MLA_HW, D_MODEL))
    p['w_out'] = _bf(W['w_out'][l])
    p['w_ff1'] = _bf(W['w_ff1'][l])
    p['w_ff2'] = _bf(W['w_ff2'][l])
    return p


def _gla_state_in(s):
    eye = jnp.eye(GLA_HEADS, dtype=F32)
    return jnp.einsum('bhdv,hg->bhvgd', s, eye).reshape(s.shape[0], GLA_W, GLA_KW)


def _gla_state_out(st):
    b = st.shape[0]
    st = st.reshape(b, GLA_HEADS, GLA_DV, GLA_HEADS, GLA_DK)
    return jnp.stack([st[:, h, :, h, :] for h in range(GLA_HEADS)], axis=1).swapaxes(2, 3)


def kernel(x_prompt, x_sample, cache_mla_latent, cache_mla_krope, page_table, state_rwkv, state_shift, state_gla, state_s5_re, state_s5_im, norm1_g, w_in, rw_mu, rw_w0, rw_w2, rw_a0, rw_a2, rw_g2, rw_v0, rw_v1, rw_v2, rw_k_k, rw_k_a, rw_r_k, rw_ln_g, rw_ln_b, gla_a2, gla_ab, gla_norm_g, s5_a_re, s5_a_im, s5_log_dt, s5_b_re, s5_b_im, s5_c_re, s5_c_im, s5_d, s5_glu_a, s5_glu_b, mla_qa_g, mla_w_uq, mla_kva_g, mla_w_uk, mla_w_uv, mla_q_g, mla_k_g, w_gate, w_branch, w_out, norm2_g, w_ff1, w_ff2):
    W = dict(norm1_g=norm1_g, w_in=w_in, rw_mu=rw_mu, rw_w0=rw_w0, rw_w2=rw_w2, rw_a0=rw_a0, rw_a2=rw_a2,
             rw_g2=rw_g2, rw_v0=rw_v0, rw_v1=rw_v1, rw_v2=rw_v2, rw_k_k=rw_k_k, rw_k_a=rw_k_a, rw_r_k=rw_r_k,
             rw_ln_g=rw_ln_g, rw_ln_b=rw_ln_b, gla_a2=gla_a2, gla_ab=gla_ab, gla_norm_g=gla_norm_g,
             s5_a_re=s5_a_re, s5_a_im=s5_a_im, s5_log_dt=s5_log_dt, s5_b_re=s5_b_re, s5_b_im=s5_b_im,
             s5_c_re=s5_c_re, s5_c_im=s5_c_im, s5_d=s5_d, s5_glu_a=s5_glu_a, s5_glu_b=s5_glu_b,
             mla_qa_g=mla_qa_g, mla_w_uq=mla_w_uq, mla_kva_g=mla_kva_g, mla_w_uk=mla_w_uk, mla_w_uv=mla_w_uv,
             mla_q_g=mla_q_g, mla_k_g=mla_k_g, w_gate=w_gate, w_branch=w_branch, w_out=w_out,
             norm2_g=norm2_g, w_ff1=w_ff1, w_ff2=w_ff2)
    bp, tp, _ = x_prompt.shape
    bs, ts, _ = x_sample.shape
    depth = w_in.shape[0]
    np_, ns = bp * tp, bs * ts
    past_len = page_table.shape[1] * PAGE_SIZE
    tpad = SUBLANE

    tables_p = rope_tables(jnp.arange(tp, dtype=jnp.int32))
    tables_new = rope_tables(past_len + jnp.arange(tpad, dtype=jnp.int32))
    tables_past = rope_tables_transposed(jnp.arange(past_len, dtype=jnp.int32))
    cache_krt = jnp.swapaxes(cache_mla_krope, 2, 3)

    xp, xs = x_prompt.reshape(np_, D_MODEL), x_sample.reshape(ns, D_MODEL)
    pad_t = lambda a: jnp.pad(a.reshape(bs, ts, a.shape[-1]), ((0, 0), (0, tpad - ts), (0, 0)))
    unpad = lambda a: a[:, :ts].reshape(ns, a.shape[-1])
    vf_p = vf_s = None
    outs = {k: [] for k in ('c_p', 'kr_p', 'c_s', 'kr_s', 'rw_p', 'rw_s', 'sh_p', 'sh_s', 'gla_p', 'gla_s',
                            're_p', 'im_p', 're_s', 'im_s')}
    zeros_p = lambda *shape: jnp.zeros((bp,) + shape, F32)
    for l in range(depth):
        p = prepare_layer(l, W)
        xn_p, prw_p, pgla_p, ps5_p, pmla_p = norm_proj(xp, p['norm1_g'], p['w_all'])
        xn_s, prw_s, pgla_s, ps5_s, pmla_s = norm_proj(xs, p['norm1_g'], p['w_all'])
        outs['sh_p'].append(xn_p.reshape(bp, tp, D_MODEL)[:, -1])
        outs['sh_s'].append(xn_s.reshape(bs, ts, D_MODEL)[:, -1])

        y_rw_p, vf_p, s_rw_p = rwkv_mixer(prw_p.reshape(bp, tp, RW_COLS), zeros_p(1, RW_COLS),
                                          zeros_p(RW_HEADS, RW_HD, RW_HD), vf_p, p, tp)
        shift_proj = small_matmul(state_shift[l], p['w_rw']).reshape(bs, 1, RW_COLS)
        y_rw_s, vf_s, s_rw_s = rwkv_mixer(pad_t(prw_s), shift_proj, state_rwkv[l], vf_s, p, ts)
        outs['rw_p'].append(s_rw_p)
        outs['rw_s'].append(s_rw_s)

        y_gla_p, s_gla_p = gla_mixer(pgla_p.reshape(bp, tp, GLA_PAD), zeros_p(GLA_W, GLA_KW), p, tp)
        y_gla_s, s_gla_s = gla_mixer(pad_t(pgla_s), _gla_state_in(state_gla[l]), p, ts)
        outs['gla_p'].append(_gla_state_out(s_gla_p))
        outs['gla_s'].append(_gla_state_out(s_gla_s))

        y_s5_p, re_p, im_p = s5_mixer(ps5_p, zeros_p(S5_STATE), zeros_p(S5_STATE), p, tp)
        y_s5_s, re_s, im_s = s5_mixer(ps5_s, state_s5_re[l].reshape(bs, S5_STATE),
                                      state_s5_im[l].reshape(bs, S5_STATE), p, ts)
        for key, val, bb in (('re_p', re_p, bp), ('im_p', im_p, bp), ('re_s', re_s, bs), ('im_s', im_s, bs)):
            outs[key].append(val.reshape(bb, S5_GROUPS, S5_P))

        y_mla_p, c_p, kr_p = mla_prompt(pmla_p, p, bp, tp, tables_p)
        y_mla_s, c_s, kr_s = mla_decode(pad_t(pmla_s), cache_mla_latent, cache_krt, page_table, l, p,
                                        tables_new, tables_past, ts)
        outs['c_p'].append(c_p.reshape(bp, tp, MLA_LORA))
        outs['kr_p'].append(kr_p.reshape(bp, tp, MLA_ROPE))
        outs['c_s'].append(c_s[:, :ts])
        outs['kr_s'].append(kr_s[:, :ts])

        xp = mlp(merge(xp, xn_p, y_rw_p.reshape(np_, RW_W), y_gla_p.reshape(np_, GLA_W), y_s5_p, y_mla_p, p), p)
        xs = mlp(merge(xs, xn_s, unpad(y_rw_s), unpad(y_gla_s), y_s5_s, unpad(y_mla_s), p), p)

    st = lambda k: jnp.stack(outs[k], axis=0)
    return (xp.reshape(bp, tp, D_MODEL), xs.reshape(bs, ts, D_MODEL),
            st('c_p'), st('kr_p'), st('c_s'), st('kr_s'),
            st('rw_p'), st('rw_s'), st('sh_p'), st('sh_s'),
            st('gla_p'), st('gla_s'), st('re_p'), st('im_p'), st('re_s'), st('im_s'))
```
